```python
import math
import jax, jax.numpy as jnp
from jax import lax
import numpy as np

D_MODEL = 1024
BATCH = 8
SEQ = 2048
DEPTH = 1

PLE_DIM = 256
GRID_W = 64
H_DIFF = 4
DH_DIFF = 64
DV_DIFF = 2 * DH_DIFF
H_NA = 8
DH_NA = 64
WIN_R = 8
WIN_C = 16
N_BUCKETS = 32
MAX_DIST = 128
Q_BLOCK = 128
W_A = H_DIFF * DV_DIFF
W_B = H_NA * DH_NA
D_MIX = W_A + W_B
SPLITS = (
    2 * H_DIFF * DH_DIFF,
    2 * H_DIFF * DH_DIFF,
    W_A,
    W_A,
    W_B,
    W_B,
    W_B,
    W_B,
)
D_IN = sum(SPLITS)
EPS = 1e-6
NEG = -1e30

kernel_name = "hybrid_diffattn_natten_encoder_layer"


def rms_norm(x, g):
    xf = x.astype(jnp.float32)
    y = xf * lax.rsqrt(jnp.mean(xf * xf, axis=-1, keepdims=True) + EPS)
    return (y * g.astype(jnp.float32)).astype(x.dtype)


def t5_bucket(rel):
    half = N_BUCKETS // 2
    max_exact = half // 2
    ret = jnp.where(rel > 0, half, 0)
    n = jnp.abs(rel)
    nf = jnp.maximum(n, 1).astype(jnp.float32)
    large = max_exact + (jnp.log(nf / max_exact) / math.log(MAX_DIST / max_exact)
                         * (half - max_exact)).astype(jnp.int32)
    large = jnp.minimum(large, half - 1)
    return ret + jnp.where(n < max_exact, n, large)


def diff_attention(q1, q2, k1, k2, v, lam, t5_bias):
    B, H, S, dh = q1.shape
    nb = S // Q_BLOCK
    scale = dh ** -0.5
    q1b = q1.reshape(B, H, nb, Q_BLOCK, dh).transpose(2, 0, 1, 3, 4)
    q2b = q2.reshape(B, H, nb, Q_BLOCK, dh).transpose(2, 0, 1, 3, 4)
    kpos = jnp.arange(S)

    def block(args):
        qa, qc, start = args
        qpos = start + jnp.arange(Q_BLOCK)
        bias = t5_bias[t5_bucket(kpos[None, :] - qpos[:, None])]
        bias = bias.transpose(2, 0, 1).astype(jnp.float32)[None]
        s1 = jnp.einsum('bhqd,bhkd->bhqk', qa, k1).astype(jnp.float32) * scale + bias
        s2 = jnp.einsum('bhqd,bhkd->bhqk', qc, k2).astype(jnp.float32) * scale + bias
        a = jax.nn.softmax(s1, axis=-1) - lam * jax.nn.softmax(s2, axis=-1)
        return jnp.einsum('bhqk,bhkd->bhqd', a.astype(v.dtype), v)

    out = lax.map(block, (q1b, q2b, jnp.arange(nb) * Q_BLOCK))
    return out.transpose(1, 2, 0, 3, 4).reshape(B, H, S, v.shape[-1])


def neighbourhood_attention(q, k, v, rpb):
    B, S, H, d = q.shape
    R = S // GRID_W
    kr = min(WIN_R, R)
    to_grid = lambda t: t.reshape(B, R, GRID_W, H, d).transpose(0, 3, 1, 2, 4)
    qg, kgrid, vgrid = to_grid(q), to_grid(k), to_grid(v)
    rows = jnp.arange(R)
    rs = jnp.clip(rows - kr // 2, 0, R - kr)
    rows_idx = rs[:, None] + jnp.arange(kr)
    kg = kgrid[:, :, rows_idx]
    vg = vgrid[:, :, rows_idx]
    cols = jnp.arange(GRID_W)
    cs = jnp.clip(cols - WIN_C // 2, 0, GRID_W - WIN_C)
    valid = (cols[None, :] >= cs[:, None]) & (cols[None, :] < cs[:, None] + WIN_C)
    dr = rows_idx - rows[:, None] + (WIN_R - 1)
    dc = jnp.clip(cols[None, :] - cols[:, None], -(WIN_C - 1), WIN_C - 1) + (WIN_C - 1)
    bias = rpb[:, dr][..., dc]
    bias = bias.transpose(0, 1, 3, 2, 4).astype(jnp.float32)
    s = jnp.einsum('bhrcd,bhrikd->bhrcik', qg, kg).astype(jnp.float32) * (d ** -0.5) + bias[None]
    s = jnp.where(valid[:, None, :], s, NEG)
    pr = jax.nn.softmax(s.reshape(B, H, R, GRID_W, kr * GRID_W), axis=-1)
    pr = pr.reshape(B, H, R, GRID_W, kr, GRID_W)
    o = jnp.einsum('bhrcik,bhrikd->bhrcd', pr.astype(v.dtype), vg)
    return o.transpose(0, 2, 3, 1, 4).reshape(B, S, H * d)


def setup_inputs(seed: int = 0) -> dict:
    key = jax.random.key(seed)
    ks = jax.random.split(key, 20)
    nrm = lambda k, shape, s: jax.random.normal(k, shape, jnp.float32) * s
    return {
        "x": nrm(ks[0], (BATCH, SEQ, D_MODEL), 1.0),
        "p": nrm(ks[1], (DEPTH, BATCH, SEQ, PLE_DIM), 1.0),
        "norm_g": 1.0 + nrm(ks[2], (DEPTH, D_MODEL), 0.02),
        "w_in": nrm(ks[3], (DEPTH, D_MODEL, D_IN), D_MODEL ** -0.5),
        "w_out": nrm(ks[4], (DEPTH, D_MIX, D_MODEL), D_MIX ** -0.5),
        "q_norm_a": 1.0 + nrm(ks[5], (DEPTH, DH_DIFF), 0.02),
        "k_norm_a": 1.0 + nrm(ks[6], (DEPTH, DH_DIFF), 0.02),
        "lam_q1": nrm(ks[7], (DEPTH, DH_DIFF), 0.1),
        "lam_k1": nrm(ks[8], (DEPTH, DH_DIFF), 0.1),
        "lam_q2": nrm(ks[9], (DEPTH, DH_DIFF), 0.1),
        "lam_k2": nrm(ks[10], (DEPTH, DH_DIFF), 0.1),
        "subln_g": 1.0 + nrm(ks[11], (DEPTH, DV_DIFF), 0.02),
        "t5_bias": nrm(ks[12], (N_BUCKETS, H_DIFF), 0.5),
        "q_norm_b": 1.0 + nrm(ks[13], (DEPTH, DH_NA), 0.02),
        "k_norm_b": 1.0 + nrm(ks[14], (DEPTH, DH_NA), 0.02),
        "na_rpb": nrm(ks[15], (DEPTH, H_NA, 2 * WIN_R - 1, 2 * WIN_C - 1), 0.5),
        "w_ple_gate": nrm(ks[16], (DEPTH, D_MODEL, D_MODEL), D_MODEL ** -0.5),
        "w_ple_proj": nrm(ks[17], (DEPTH, PLE_DIM, D_MODEL), 0.5 * PLE_DIM ** -0.5),
    }


def reference(x, p, norm_g, w_in, w_out, q_norm_a, k_norm_a, lam_q1, lam_k1,
              lam_q2, lam_k2, subln_g, t5_bias, q_norm_b, k_norm_b, na_rpb,
              w_ple_gate, w_ple_proj):
    B, S, _ = x.shape
    cuts = list(np.cumsum(SPLITS)[:-1])
    for i in range(DEPTH):
        xn = rms_norm(x, norm_g[i])
        proj = jnp.einsum('bsd,de->bse', xn, w_in[i])
        qa, ka, va, za, qb, kb, vb, zb = jnp.split(proj, cuts, axis=-1)

        qa = rms_norm(qa.reshape(B, S, H_DIFF, 2, DH_DIFF), q_norm_a[i])
        ka = rms_norm(ka.reshape(B, S, H_DIFF, 2, DH_DIFF), k_norm_a[i])
        q1 = qa[..., 0, :].transpose(0, 2, 1, 3)
        q2 = qa[..., 1, :].transpose(0, 2, 1, 3)
        k1 = ka[..., 0, :].transpose(0, 2, 1, 3)
        k2 = ka[..., 1, :].transpose(0, 2, 1, 3)
        va_h = va.reshape(B, S, H_DIFF, DV_DIFF).transpose(0, 2, 1, 3)
        lam_init = 0.8 - 0.6 * math.exp(-0.3 * i)
        lam = (jnp.exp(jnp.sum(lam_q1[i].astype(jnp.float32) * lam_k1[i].astype(jnp.float32)))
               - jnp.exp(jnp.sum(lam_q2[i].astype(jnp.float32) * lam_k2[i].astype(jnp.float32)))
               + lam_init)
        oa = diff_attention(q1, q2, k1, k2, va_h, lam, t5_bias)
        oa = rms_norm(oa, subln_g[i]) * (1.0 - lam_init)
        ya = oa.transpose(0, 2, 1, 3).reshape(B, S, W_A) * jax.nn.silu(za)

        qb = rms_norm(qb.reshape(B, S, H_NA, DH_NA), q_norm_b[i])
        kb = rms_norm(kb.reshape(B, S, H_NA, DH_NA), k_norm_b[i])
        vb = vb.reshape(B, S, H_NA, DH_NA)
        yb = neighbourhood_attention(qb, kb, vb, na_rpb[i]) * jax.nn.silu(zb)

        y = jnp.einsum('bse,ed->bsd', jnp.concatenate([ya, yb], axis=-1), w_out[i])
        x = x + y

        gate = jax.nn.sigmoid(jnp.einsum('bsd,de->bse', x, w_ple_gate[i]))
        x = x + gate * jnp.einsum('bsp,pd->bsd', p[i], w_ple_proj[i])
    return x
```

```python
import math

import jax
import jax.numpy as jnp
import numpy as np
from jax import lax
from jax.experimental import pallas as pl
from jax.experimental.pallas import tpu as pltpu

D_MODEL = 1024
PLE_DIM = 256
GRID_W = 64
H_DIFF = 4
DH_DIFF = 64
DV_DIFF = 2 * DH_DIFF
H_NA = 8
DH_NA = 64
WIN_R = 8
WIN_C = 16
N_BUCKETS = 32
MAX_DIST = 128
W_A = H_DIFF * DV_DIFF
W_B = H_NA * DH_NA
D_IN = 4 * W_A + 4 * W_B
EPS = 1e-6
NEG = -1e30

GROUP_W = 512
NORM_GROUPS = (0, 1, 4, 5)
GATE_GROUPS = (3, 7)
HEAD_DIM = 64

TM_PROJ = 512
TM_OUT = 512
TQ = 256
NA_ROWS = 4
VMEM_LIMIT = 56 * 1024 * 1024

_F32 = jnp.float32
_BF16 = jnp.bfloat16


def _t5_thresholds():
    half = N_BUCKETS // 2
    max_exact = half // 2
    steps = half - max_exact
    ratio = MAX_DIST // max_exact
    thr = []
    for j in range(1, steps):
        n = max_exact
        while n ** steps < (max_exact ** steps) * (ratio ** j):
            n += 1
        thr.append(n)
    return tuple(thr)


def _t5_table_kernel(tb_ref, o_ref, *, seq, blk_w):
    h = pl.program_id(0)
    c0 = pl.program_id(1) * blk_w
    half = N_BUCKETS // 2
    max_exact = half // 2
    ql = lax.broadcasted_iota(jnp.int32, (TQ, blk_w), 0)
    c = lax.broadcasted_iota(jnp.int32, (TQ, blk_w), 1) + c0
    rel = c - (seq - TQ) - ql
    n = jnp.abs(rel)
    large = jnp.full_like(n, max_exact)
    for thr in _t5_thresholds():
        large = large + (n >= thr).astype(jnp.int32)
    bucket = jnp.where(rel > 0, half, 0) + jnp.where(n < max_exact, n, large)
    val = jnp.zeros((TQ, blk_w), _F32)
    for j in range(N_BUCKETS):
        val = jnp.where(bucket == j, tb_ref[j * H_DIFF + h], val)
    o_ref[...] = val


def _t5_table(t5_bias, seq):
    blk_w = 512
    width = pl.cdiv(2 * seq - TQ, blk_w) * blk_w
    return pl.pallas_call(
        lambda tb, o: _t5_table_kernel(tb, o, seq=seq, blk_w=blk_w),
        grid=(H_DIFF, width // blk_w),
        in_specs=[pl.BlockSpec(memory_space=pltpu.SMEM)],
        out_specs=pl.BlockSpec((None, TQ, blk_w), lambda h, j: (h, 0, j)),
        out_shape=jax.ShapeDtypeStruct((H_DIFF, TQ, width), _F32),
        name="t5_table",
    )(t5_bias.reshape(-1))


def _na_table_kernel(rpb_ref, o_ref, *, kr):
    h = pl.program_id(0)
    n_dr = 2 * WIN_R - 1
    n_dc = 2 * WIN_C - 1
    shape = (GRID_W, kr * GRID_W)
    cq = lax.broadcasted_iota(jnp.int32, shape, 0)
    lane = lax.broadcasted_iota(jnp.int32, shape, 1)
    ck = lane & (GRID_W - 1)
    dc = jnp.clip(ck - cq, -(WIN_C - 1), WIN_C - 1) + (WIN_C - 1)
    cs = jnp.clip(cq - WIN_C // 2, 0, GRID_W - WIN_C)
    valid = (ck >= cs) & (ck < cs + WIN_C)
    i_row = lax.broadcasted_iota(jnp.int32, (1, kr * GRID_W), 1) >> (GRID_W.bit_length() - 1)
    for t in range(kr):
        acc = jnp.zeros(shape, _F32)
        for j in range(n_dc):
            rj = jnp.zeros((1, kr * GRID_W), _F32)
            for i in range(kr):
                dr = i - t + (WIN_R - 1)
                rj = jnp.where(i_row == i, rpb_ref[(h * n_dr + dr) * n_dc + j], rj)
            acc = jnp.where(dc == j, rj, acc)
        o_ref[t] = jnp.where(valid, acc, NEG)


def _na_table(rpb, kr):
    return pl.pallas_call(
        lambda r, o: _na_table_kernel(r, o, kr=kr),
        grid=(H_NA,),
        in_specs=[pl.BlockSpec(memory_space=pltpu.SMEM)],
        out_specs=pl.BlockSpec((None, kr, GRID_W, kr * GRID_W), lambda h: (h, 0, 0, 0)),
        out_shape=jax.ShapeDtypeStruct((H_NA, kr, GRID_W, kr * GRID_W), _F32),
        name="na_table",
    )(rpb.reshape(-1))


def _proj_kernel(x_ref, ng_ref, w_ref, seg_ref, gain_ref, o_ref):
    x = x_ref[...]
    ms = jnp.mean(x * x, axis=-1, keepdims=True)
    xn = (x * lax.rsqrt(ms + EPS) * ng_ref[...]).astype(_BF16)
    seg = seg_ref[...]
    half_w = GROUP_W // 2
    for j in range(D_IN // GROUP_W):
        sl = slice(j * GROUP_W, (j + 1) * GROUP_W)
        acc = jnp.dot(xn, w_ref[:, sl], preferred_element_type=_F32)
        if j in NORM_GROUPS:
            sq = (acc * acc).astype(_BF16)
            ss = jnp.concatenate(
                [jnp.dot(sq[:, :half_w], seg, preferred_element_type=_F32),
                 jnp.dot(sq[:, half_w:], seg, preferred_element_type=_F32)], axis=-1)
            acc = acc * lax.rsqrt(ss * (1.0 / HEAD_DIM) + EPS) * gain_ref[:, sl]
        elif j in GATE_GROUPS:
            acc = acc * jax.nn.sigmoid(acc)
        o_ref[:, sl] = acc.astype(_BF16)


def _proj(x2, norm_g, w_bf, seg, gains):
    n = x2.shape[0]
    return pl.pallas_call(
        _proj_kernel,
        grid=(n // TM_PROJ,),
        in_specs=[
            pl.BlockSpec((TM_PROJ, D_MODEL), lambda i: (i, 0)),
            pl.BlockSpec((1, D_MODEL), lambda i: (0, 0)),
            pl.BlockSpec((D_MODEL, D_IN), lambda i: (0, 0)),
            pl.BlockSpec(seg.shape, lambda i: (0, 0)),
            pl.BlockSpec((1, D_IN), lambda i: (0, 0)),
        ],
        out_specs=pl.BlockSpec((TM_PROJ, D_IN), lambda i: (i, 0)),
        out_shape=jax.ShapeDtypeStruct((n, D_IN), _BF16),
        compiler_params=pltpu.CompilerParams(
            dimension_semantics=("arbitrary",), vmem_limit_bytes=VMEM_LIMIT),
        name="proj",
    )(x2, norm_g, w_bf, seg, gains)


def _diff_kernel(lam_ref, q_ref, k_ref, v_ref, z_ref, bias_ref, g_ref, o_ref,
                 k1_scr, k2_scr, *, seq, lam_init):
    i = pl.program_id(2)

    @pl.when(i == 0)
    def _():
        k = k_ref[...]
        lane = lax.broadcasted_iota(jnp.int32, k.shape, 1)
        zero = jnp.zeros_like(k)
        k1_scr[...] = jnp.where(lane < DH_DIFF, k, zero)
        k2_scr[...] = jnp.where(lane >= DH_DIFF, k, zero)

    lv = lam_ref[...]
    lam = (jnp.exp(jnp.sum(lv[0:1] * lv[1:2], axis=-1, keepdims=True))
           - jnp.exp(jnp.sum(lv[2:3] * lv[3:4], axis=-1, keepdims=True)) + lam_init)

    q = q_ref[...]
    off = pl.multiple_of(seq - TQ - i * TQ, 128)
    bias = bias_ref[:, pl.ds(off, seq)]
    nt = (((1,), (1,)), ((), ()))
    s1 = lax.dot_general(q, k1_scr[...], nt, preferred_element_type=_F32) + bias
    s2 = lax.dot_general(q, k2_scr[...], nt, preferred_element_type=_F32) + bias
    e1 = jnp.exp(s1 - jnp.max(s1, axis=-1, keepdims=True))
    e2 = jnp.exp(s2 - jnp.max(s2, axis=-1, keepdims=True))
    c1 = 1.0 / jnp.sum(e1, axis=-1, keepdims=True)
    c2 = lam / jnp.sum(e2, axis=-1, keepdims=True)
    a = (e1 * c1 - e2 * c2).astype(_BF16)
    o = jnp.dot(a, v_ref[...], preferred_element_type=_F32)
    ms = jnp.mean(o * o, axis=-1, keepdims=True)
    o = o * lax.rsqrt(ms + EPS) * g_ref[...] * (1.0 - lam_init)
    o_ref[...] = (o * z_ref[...].astype(_F32)).astype(_BF16)


def _diff_attention(proj3, lam_vecs, bias_tab, subln_g, lam_init):
    b, seq, _ = proj3.shape
    width = bias_tab.shape[-1]
    blk = lambda col0: (lambda bi, h, i: (bi, i, col0 + h))
    full = lambda col0: (lambda bi, h, i: (bi, 0, col0 + h))
    return pl.pallas_call(
        lambda *refs: _diff_kernel(*refs, seq=seq, lam_init=lam_init),
        grid=(b, H_DIFF, seq // TQ),
        in_specs=[
            pl.BlockSpec(lam_vecs.shape, lambda bi, h, i: (0, 0)),
            pl.BlockSpec((None, TQ, DV_DIFF), blk(0)),
            pl.BlockSpec((None, seq, DV_DIFF), full(H_DIFF)),
            pl.BlockSpec((None, seq, DV_DIFF), full(2 * H_DIFF)),
            pl.BlockSpec((None, TQ, DV_DIFF), blk(3 * H_DIFF)),
            pl.BlockSpec((None, TQ, width), lambda bi, h, i: (h, 0, 0)),
            pl.BlockSpec((1, DV_DIFF), lambda bi, h, i: (0, 0)),
        ],
        out_specs=pl.BlockSpec((None, TQ, DV_DIFF), lambda bi, h, i: (bi, i, h)),
        out_shape=jax.ShapeDtypeStruct((b, seq, W_A), _BF16),
        scratch_shapes=[pltpu.VMEM((seq, DV_DIFF), _BF16), pltpu.VMEM((seq, DV_DIFF), _BF16)],
        compiler_params=pltpu.CompilerParams(
            dimension_semantics=("arbitrary", "arbitrary", "arbitrary"),
            vmem_limit_bytes=VMEM_LIMIT),
        name="diff_attn",
    )(lam_vecs, proj3, proj3, proj3, proj3, bias_tab, subln_g)


def _nbr_kernel(q_ref, k_ref, v_ref, z_ref, bm_ref, o_ref, ke_scr, ko_scr, *, rows, kr):
    g = pl.program_id(1)
    pair_w = 2 * DH_NA

    @pl.when(g == 0)
    def _():
        k = k_ref[...]
        lane = lax.broadcasted_iota(jnp.int32, k.shape, 1) & (pair_w - 1)
        zero = jnp.zeros_like(k)
        ke_scr[...] = jnp.where(lane < DH_NA, k, zero)
        ko_scr[...] = jnp.where(lane >= DH_NA, k, zero)

    nt = (((1,), (1,)), ((), ()))
    win = kr * GRID_W
    out_lane = lax.broadcasted_iota(jnp.int32, (GRID_W, pair_w), 1)
    for rr in range(NA_ROWS):
        r = g * NA_ROWS + rr
        rs = jnp.clip(r - kr // 2, 0, rows - kr)
        t = r - rs
        start = pl.multiple_of(rs * GRID_W, GRID_W)
        rsl = slice(rr * GRID_W, (rr + 1) * GRID_W)
        for hp in range(H_NA // 2):
            csl = slice(hp * pair_w, (hp + 1) * pair_w)
            qp = q_ref[rsl, csl]
            vw = v_ref[pl.ds(start, win), csl]
            halves = []
            for par, kscr in ((0, ke_scr), (1, ko_scr)):
                kw = kscr[pl.ds(start, win), csl]
                s = lax.dot_general(qp, kw, nt, preferred_element_type=_F32)
                s = s + bm_ref[2 * hp + par, t]
                e = jnp.exp(s - jnp.max(s, axis=-1, keepdims=True))
                inv = 1.0 / jnp.sum(e, axis=-1, keepdims=True)
                halves.append(jnp.dot(e.astype(_BF16), vw, preferred_element_type=_F32) * inv)
            o = jnp.where(out_lane < DH_NA, halves[0], halves[1])
            o_ref[rsl, csl] = (o * z_ref[rsl, csl].astype(_F32)).astype(_BF16)


def _nbr_attention(proj3, bm_tab):
    b, seq, _ = proj3.shape
    rows = seq // GRID_W
    kr = bm_tab.shape[1]
    tok = NA_ROWS * GRID_W
    grp = lambda gi: (lambda bi, g: (bi, g, gi))
    full = lambda gi: (lambda bi, g: (bi, 0, gi))
    return pl.pallas_call(
        lambda *refs: _nbr_kernel(*refs, rows=rows, kr=kr),
        grid=(b, rows // NA_ROWS),
        in_specs=[
            pl.BlockSpec((None, tok, W_B), grp(4)),
            pl.BlockSpec((None, seq, W_B), full(5)),
            pl.BlockSpec((None, seq, W_B), full(6)),
            pl.BlockSpec((None, tok, W_B), grp(7)),
            pl.BlockSpec(bm_tab.shape, lambda bi, g: (0, 0, 0, 0)),
        ],
        out_specs=pl.BlockSpec((None, tok, W_B), lambda bi, g: (bi, g, 0)),
        out_shape=jax.ShapeDtypeStruct((b, seq, W_B), _BF16),
        scratch_shapes=[pltpu.VMEM((seq, W_B), _BF16), pltpu.VMEM((seq, W_B), _BF16)],
        compiler_params=pltpu.CompilerParams(
            dimension_semantics=("arbitrary", "arbitrary"), vmem_limit_bytes=VMEM_LIMIT),
        name="nbr_attn",
    )(proj3, proj3, proj3, proj3, bm_tab)


def _out_kernel(x_ref, ya_ref, yb_ref, p_ref, wo_ref, wg_ref, wp_ref, o_ref):
    y = (jnp.dot(ya_ref[...], wo_ref[:W_A, :], preferred_element_type=_F32)
         + jnp.dot(yb_ref[...], wo_ref[W_A:, :], preferred_element_type=_F32))
    x1 = x_ref[...] + y
    gate = jax.nn.sigmoid(jnp.dot(x1.astype(_BF16), wg_ref[...], preferred_element_type=_F32))
    ple = jnp.dot(p_ref[...].astype(_BF16), wp_ref[...], preferred_element_type=_F32)
    o_ref[...] = x1 + gate * ple


def _out(x2, ya2, yb2, p2, wo, wg, wp):
    n = x2.shape[0]
    tile = lambda w: pl.BlockSpec((TM_OUT, w), lambda i: (i, 0))
    whole = lambda a: pl.BlockSpec(a.shape, lambda i: (0, 0))
    return pl.pallas_call(
        _out_kernel,
        grid=(n // TM_OUT,),
        in_specs=[tile(D_MODEL), tile(W_A), tile(W_B), tile(PLE_DIM),
                  whole(wo), whole(wg), whole(wp)],
        out_specs=tile(D_MODEL),
        out_shape=jax.ShapeDtypeStruct((n, D_MODEL), _F32),
        compiler_params=pltpu.CompilerParams(
            dimension_semantics=("arbitrary",), vmem_limit_bytes=VMEM_LIMIT),
        name="out_proj",
    )(x2, ya2, yb2, p2, wo, wg, wp)


def _segment_ones():
    idx = np.arange(GROUP_W // 2) // HEAD_DIM
    return jnp.asarray((idx[:, None] == idx[None, :]).astype(np.float32), dtype=_BF16)


def kernel(x, p, norm_g, w_in, w_out, q_norm_a, k_norm_a, lam_q1, lam_k1, lam_q2, lam_k2,
           subln_g, t5_bias, q_norm_b, k_norm_b, na_rpb, w_ple_gate, w_ple_proj):
    b, seq, _ = x.shape
    depth = w_in.shape[0]
    n = b * seq
    rows = seq // GRID_W
    kr = min(WIN_R, rows)
    assert kr == WIN_R and rows % NA_ROWS == 0 and seq % TQ == 0 and n % TM_PROJ == 0

    seg = _segment_ones()
    bias_tab = _t5_table(t5_bias.astype(_F32), seq)
    ones_w = jnp.ones((1, GROUP_W), _F32)
    for i in range(depth):
        lam_init = 0.8 - 0.6 * math.exp(-0.3 * i)
        tile_a = lambda g, s: jnp.tile(g.astype(_F32), GROUP_W // DH_DIFF)[None, :] * s
        gains = jnp.concatenate([
            tile_a(q_norm_a[i], DH_DIFF ** -0.5), tile_a(k_norm_a[i], 1.0), ones_w, ones_w,
            tile_a(q_norm_b[i], DH_NA ** -0.5), tile_a(k_norm_b[i], 1.0), ones_w, ones_w], axis=-1)
        proj = _proj(x.reshape(n, D_MODEL), norm_g[i][None, :].astype(_F32),
                     w_in[i].astype(_BF16), seg, gains)
        proj3 = proj.reshape(b, seq, D_IN)
        lam_vecs = jnp.stack([lam_q1[i], lam_k1[i], lam_q2[i], lam_k2[i]]).astype(_F32)
        ya = _diff_attention(proj3, lam_vecs, bias_tab, subln_g[i][None, :].astype(_F32), lam_init)
        bm_tab = _na_table(na_rpb[i].astype(_F32), kr)
        yb = _nbr_attention(proj3, bm_tab)
        x = _out(x.reshape(n, D_MODEL), ya.reshape(n, W_A), yb.reshape(n, W_B),
                 p[i].reshape(n, PLE_DIM), w_out[i].astype(_BF16),
                 w_ple_gate[i].astype(_BF16), w_ple_proj[i].astype(_BF16)).reshape(b, seq, D_MODEL)
    return x
```

```python
import math

import jax
import jax.numpy as jnp
import numpy as np
from jax import lax
from jax.experimental import pallas as pl
from jax.experimental.pallas import tpu as pltpu

D_MODEL = 1024
PLE_DIM = 256
GRID_W = 64
H_DIFF = 4
DH_DIFF = 64
DV_DIFF = 2 * DH_DIFF
H_NA = 8
DH_NA = 64
WIN_R = 8
WIN_C = 16
N_BUCKETS = 32
MAX_DIST = 128
W_A = H_DIFF * DV_DIFF
W_B = H_NA * DH_NA
D_IN = 4 * W_A + 4 * W_B
EPS = 1e-6
NEG = -1e30

GROUP_W = 512
NORM_GROUPS = (0, 1, 4, 5)
GATE_GROUPS = (3, 7)
HEAD_DIM = 64

TM_PROJ = 512
TM_OUT = 512
TQ = 256
NA_ROWS = 4
NA_KROWS = NA_ROWS + WIN_R
LOG2E = math.log2(math.e)
VMEM_LIMIT = 56 * 1024 * 1024

_F32 = jnp.float32
_BF16 = jnp.bfloat16


def _t5_thresholds():
    half = N_BUCKETS // 2
    max_exact = half // 2
    steps = half - max_exact
    ratio = MAX_DIST // max_exact
    thr = []
    for j in range(1, steps):
        n = max_exact
        while n ** steps < (max_exact ** steps) * (ratio ** j):
            n += 1
        thr.append(n)
    return tuple(thr)


def _t5_table_kernel(tb_ref, o_ref, *, seq, blk_w):
    h = pl.program_id(0)
    c0 = pl.program_id(1) * blk_w
    half = N_BUCKETS // 2
    max_exact = half // 2
    all_far_neg = c0 + (blk_w - 1) - (seq - TQ) <= -MAX_DIST
    all_far_pos = c0 - (seq - TQ) - (TQ - 1) >= MAX_DIST

    @pl.when(all_far_neg)
    def _():
        o_ref[...] = jnp.full((TQ, blk_w), tb_ref[(half - 1) * H_DIFF + h] * LOG2E, _F32)

    @pl.when(all_far_pos)
    def _():
        o_ref[...] = jnp.full((TQ, blk_w), tb_ref[(N_BUCKETS - 1) * H_DIFF + h] * LOG2E, _F32)

    @pl.when(jnp.logical_not(jnp.logical_or(all_far_neg, all_far_pos)))
    def _():
        ql = lax.broadcasted_iota(jnp.int32, (TQ, blk_w), 0)
        c = lax.broadcasted_iota(jnp.int32, (TQ, blk_w), 1) + c0
        rel = c - (seq - TQ) - ql
        n = jnp.abs(rel)
        large = jnp.full_like(n, max_exact)
        for thr in _t5_thresholds():
            large = large + (n >= thr).astype(jnp.int32)
        bucket = jnp.where(rel > 0, half, 0) + jnp.where(n < max_exact, n, large)
        val = jnp.zeros((TQ, blk_w), _F32)
        for j in range(N_BUCKETS):
            val = jnp.where(bucket == j, tb_ref[j * H_DIFF + h] * LOG2E, val)
        o_ref[...] = val


def _t5_table(t5_bias, seq):
    blk_w = 512
    width = pl.cdiv(2 * seq - TQ, blk_w) * blk_w
    return pl.pallas_call(
        lambda tb, o: _t5_table_kernel(tb, o, seq=seq, blk_w=blk_w),
        grid=(H_DIFF, width // blk_w),
        in_specs=[pl.BlockSpec(memory_space=pltpu.SMEM)],
        out_specs=pl.BlockSpec((None, TQ, blk_w), lambda h, j: (h, 0, j)),
        out_shape=jax.ShapeDtypeStruct((H_DIFF, TQ, width), _F32),
        name="t5_table",
    )(t5_bias.reshape(-1))


def _na_first_key_row(r, rows, kr):
    return min(max(r - kr // 2, 0), rows - kr)


def _na_window_start(g, rows, kr):
    return min(max(g * NA_ROWS - kr // 2, 0), rows - NA_KROWS)


def _na_variant_groups(rows):
    return (0, 1, rows // NA_ROWS - 1)


def _na_table_kernel(rpb_ref, o_ref, *, rows, kr):
    h = pl.program_id(0)
    n_dr = 2 * WIN_R - 1
    n_dc = 2 * WIN_C - 1
    pair = (GRID_W, 2 * GRID_W)
    cq = lax.broadcasted_iota(jnp.int32, pair, 0)
    lane = lax.broadcasted_iota(jnp.int32, pair, 1)
    ck = lane & (GRID_W - 1)
    dc = jnp.clip(ck - cq, -(WIN_C - 1), WIN_C - 1) + (WIN_C - 1)
    cs = jnp.clip(cq - WIN_C // 2, 0, GRID_W - WIN_C)
    col_ok = (ck >= cs) & (ck < cs + WIN_C)
    left = lane < GRID_W
    neg = jnp.full(pair, NEG, _F32)
    tiles = {}

    def tile(dr):
        if dr not in tiles:
            acc = jnp.zeros(pair, _F32)
            for j in range(n_dc):
                acc = jnp.where(dc == j, rpb_ref[(h * n_dr + dr) * n_dc + j] * LOG2E, acc)
            tiles[dr] = jnp.where(col_ok, acc, NEG)
        return tiles[dr]

    for v, g in enumerate(_na_variant_groups(rows)):
        ws = _na_window_start(g, rows, kr)
        for rr in range(NA_ROWS):
            r = g * NA_ROWS + rr
            rs = _na_first_key_row(r, rows, kr)
            for kp in range(NA_KROWS // 2):
                parts = []
                for ko in (2 * kp, 2 * kp + 1):
                    in_win = 0 <= ws + ko - rs < kr
                    parts.append(tile(ws + ko - r + WIN_R - 1) if in_win else neg)
                val = parts[0] if parts[0] is parts[1] else jnp.where(left, parts[0], parts[1])
                o_ref[v, rr * GRID_W:(rr + 1) * GRID_W, kp * 2 * GRID_W:(kp + 1) * 2 * GRID_W] = val


def _na_table(rpb, rows, kr):
    n_var = len(_na_variant_groups(rows))
    shape = (n_var, H_NA, NA_ROWS * GRID_W, NA_KROWS * GRID_W)
    return pl.pallas_call(
        lambda r, o: _na_table_kernel(r, o, rows=rows, kr=kr),
        grid=(H_NA,),
        in_specs=[pl.BlockSpec(memory_space=pltpu.SMEM)],
        out_specs=pl.BlockSpec((n_var, None) + shape[2:], lambda h: (0, h, 0, 0)),
        out_shape=jax.ShapeDtypeStruct(shape, _F32),
        name="na_table",
    )(rpb.reshape(-1))


def _proj_kernel(x_ref, ng_ref, w_ref, seg_ref, gain_ref, o_ref):
    x = x_ref[...]
    ms = jnp.mean(x * x, axis=-1, keepdims=True)
    xn = (x * lax.rsqrt(ms + EPS) * ng_ref[...]).astype(_BF16)
    seg = seg_ref[...]
    half_w = GROUP_W // 2
    for j in range(D_IN // GROUP_W):
        sl = slice(j * GROUP_W, (j + 1) * GROUP_W)
        acc = jnp.dot(xn, w_ref[:, sl], preferred_element_type=_F32)
        if j in NORM_GROUPS:
            sq = (acc * acc).astype(_BF16)
            ss = jnp.concatenate(
                [jnp.dot(sq[:, :half_w], seg, preferred_element_type=_F32),
                 jnp.dot(sq[:, half_w:], seg, preferred_element_type=_F32)], axis=-1)
            acc = acc * lax.rsqrt(ss * (1.0 / HEAD_DIM) + EPS) * gain_ref[:, sl]
        elif j in GATE_GROUPS:
            acc = acc * jax.nn.sigmoid(acc)
        o_ref[:, sl] = acc.astype(_BF16)


def _proj(x2, norm_g, w_bf, seg, gains):
    n = x2.shape[0]
    return pl.pallas_call(
        _proj_kernel,
        grid=(n // TM_PROJ,),
        in_specs=[
            pl.BlockSpec((TM_PROJ, D_MODEL), lambda i: (i, 0)),
            pl.BlockSpec((1, D_MODEL), lambda i: (0, 0)),
            pl.BlockSpec((D_MODEL, D_IN), lambda i: (0, 0)),
            pl.BlockSpec(seg.shape, lambda i: (0, 0)),
            pl.BlockSpec((1, D_IN), lambda i: (0, 0)),
        ],
        out_specs=pl.BlockSpec((TM_PROJ, D_IN), lambda i: (i, 0)),
        out_shape=jax.ShapeDtypeStruct((n, D_IN), _BF16),
        compiler_params=pltpu.CompilerParams(
            dimension_semantics=("arbitrary",), vmem_limit_bytes=VMEM_LIMIT),
        name="proj",
    )(x2, norm_g, w_bf, seg, gains)


def _diff_kernel(lam_ref, q_ref, k_ref, v_ref, z_ref, bias_ref, g_ref, o_ref,
                 k1_scr, k2_scr, *, seq, lam_init):
    i = pl.program_id(2)

    @pl.when(i == 0)
    def _():
        k = k_ref[...]
        lane = lax.broadcasted_iota(jnp.int32, k.shape, 1)
        zero = jnp.zeros_like(k)
        k1_scr[...] = jnp.where(lane < DH_DIFF, k, zero)
        k2_scr[...] = jnp.where(lane >= DH_DIFF, k, zero)

    lv = lam_ref[...]
    lam = (jnp.exp(jnp.sum(lv[0:1] * lv[1:2], axis=-1, keepdims=True))
           - jnp.exp(jnp.sum(lv[2:3] * lv[3:4], axis=-1, keepdims=True)) + lam_init)

    q = q_ref[...]
    off = pl.multiple_of(seq - TQ - i * TQ, 128)
    bias = bias_ref[:, pl.ds(off, seq)]
    nt = (((1,), (1,)), ((), ()))
    s1 = lax.dot_general(q, k1_scr[...], nt, preferred_element_type=_F32) + bias
    s2 = lax.dot_general(q, k2_scr[...], nt, preferred_element_type=_F32) + bias
    e1 = jnp.exp2(s1 - jnp.max(s1, axis=-1, keepdims=True))
    e2 = jnp.exp2(s2 - jnp.max(s2, axis=-1, keepdims=True))
    c1 = 1.0 / jnp.sum(e1, axis=-1, keepdims=True)
    c2 = lam / jnp.sum(e2, axis=-1, keepdims=True)
    a = (e1 * c1 - e2 * c2).astype(_BF16)
    o = jnp.dot(a, v_ref[...], preferred_element_type=_F32)
    ms = jnp.mean(o * o, axis=-1, keepdims=True)
    o = o * lax.rsqrt(ms + EPS) * g_ref[...] * (1.0 - lam_init)
    o_ref[...] = (o * z_ref[...].astype(_F32)).astype(_BF16)


def _diff_attention(proj3, lam_vecs, bias_tab, subln_g, lam_init):
    b, seq, _ = proj3.shape
    width = bias_tab.shape[-1]
    blk = lambda col0: (lambda bi, h, i: (bi, i, col0 + h))
    full = lambda col0: (lambda bi, h, i: (bi, 0, col0 + h))
    return pl.pallas_call(
        lambda *refs: _diff_kernel(*refs, seq=seq, lam_init=lam_init),
        grid=(b, H_DIFF, seq // TQ),
        in_specs=[
            pl.BlockSpec(lam_vecs.shape, lambda bi, h, i: (0, 0)),
            pl.BlockSpec((None, TQ, DV_DIFF), blk(0)),
            pl.BlockSpec((None, seq, DV_DIFF), full(H_DIFF)),
            pl.BlockSpec((None, seq, DV_DIFF), full(2 * H_DIFF)),
            pl.BlockSpec((None, TQ, DV_DIFF), blk(3 * H_DIFF)),
            pl.BlockSpec((None, TQ, width), lambda bi, h, i: (h, 0, 0)),
            pl.BlockSpec((1, DV_DIFF), lambda bi, h, i: (0, 0)),
        ],
        out_specs=pl.BlockSpec((None, TQ, DV_DIFF), lambda bi, h, i: (bi, i, h)),
        out_shape=jax.ShapeDtypeStruct((b, seq, W_A), _BF16),
        scratch_shapes=[pltpu.VMEM((seq, DV_DIFF), _BF16), pltpu.VMEM((seq, DV_DIFF), _BF16)],
        compiler_params=pltpu.CompilerParams(
            dimension_semantics=("arbitrary", "arbitrary", "arbitrary"),
            vmem_limit_bytes=VMEM_LIMIT),
        name="diff_attn",
    )(lam_vecs, proj3, proj3, proj3, proj3, bias_tab, subln_g)


def _nbr_kernel(q_ref, k_ref, v_ref, z_ref, bm_ref, o_ref, ke_scr, ko_scr, *, rows, kr):
    g = pl.program_id(1)
    pair_w = 2 * DH_NA

    @pl.when(g == 0)
    def _():
        k = k_ref[...]
        lane = lax.broadcasted_iota(jnp.int32, k.shape, 1) & (pair_w - 1)
        zero = jnp.zeros_like(k)
        ke_scr[...] = jnp.where(lane < DH_NA, k, zero)
        ko_scr[...] = jnp.where(lane >= DH_NA, k, zero)

    nt = (((1,), (1,)), ((), ()))
    win = NA_KROWS * GRID_W
    tok = NA_ROWS * GRID_W
    ws = jnp.clip(g * NA_ROWS - kr // 2, 0, rows - NA_KROWS)
    start = pl.multiple_of(ws * GRID_W, GRID_W)
    out_lane = lax.broadcasted_iota(jnp.int32, (tok, pair_w), 1)
    for hp in range(H_NA // 2):
        csl = slice(hp * pair_w, (hp + 1) * pair_w)
        qp = q_ref[:, csl]
        vw = v_ref[pl.ds(start, win), csl]
        halves = []
        for par, kscr in ((0, ke_scr), (1, ko_scr)):
            kw = kscr[pl.ds(start, win), csl]
            s = lax.dot_general(qp, kw, nt, preferred_element_type=_F32)
            s = s + bm_ref[2 * hp + par]
            e = jnp.exp2(s - jnp.max(s, axis=-1, keepdims=True))
            inv = 1.0 / jnp.sum(e, axis=-1, keepdims=True)
            halves.append(jnp.dot(e.astype(_BF16), vw, preferred_element_type=_F32) * inv)
        o = jnp.where(out_lane < DH_NA, halves[0], halves[1])
        o_ref[:, csl] = (o * z_ref[:, csl].astype(_F32)).astype(_BF16)


def _nbr_attention(proj3, bm_tab, kr):
    b, seq, _ = proj3.shape
    rows = seq // GRID_W
    n_grp = rows // NA_ROWS
    tok = NA_ROWS * GRID_W
    sig = lambda g: tuple(
        (0 <= _na_window_start(g, rows, kr) + ko - _na_first_key_row(g * NA_ROWS + rr, rows, kr) < kr,
         _na_window_start(g, rows, kr) + ko - (g * NA_ROWS + rr))
        for rr in range(NA_ROWS) for ko in range(NA_KROWS))
    variant_of = lambda g: (g > 0) + (g == n_grp - 1)
    reps = _na_variant_groups(rows)
    assert all(sig(g) == sig(reps[variant_of(g)]) for g in range(n_grp))
    grp = lambda gi: (lambda bi, g: (bi, g, gi))
    full = lambda gi: (lambda bi, g: (bi, 0, gi))
    return pl.pallas_call(
        lambda *refs: _nbr_kernel(*refs, rows=rows, kr=kr),
        grid=(b, n_grp),
        in_specs=[
            pl.BlockSpec((None, tok, W_B), grp(4)),
            pl.BlockSpec((None, seq, W_B), full(5)),
            pl.BlockSpec((None, seq, W_B), full(6)),
            pl.BlockSpec((None, tok, W_B), grp(7)),
            pl.BlockSpec((None,) + bm_tab.shape[1:],
                         lambda bi, g: ((g > 0).astype(jnp.int32) + (g == n_grp - 1).astype(jnp.int32),
                                        0, 0, 0)),
        ],
        out_specs=pl.BlockSpec((None, tok, W_B), lambda bi, g: (bi, g, 0)),
        out_shape=jax.ShapeDtypeStruct((b, seq, W_B), _BF16),
        scratch_shapes=[pltpu.VMEM((seq, W_B), _BF16), pltpu.VMEM((seq, W_B), _BF16)],
        compiler_params=pltpu.CompilerParams(
            dimension_semantics=("arbitrary", "arbitrary"), vmem_limit_bytes=VMEM_LIMIT),
        name="nbr_attn",
    )(proj3, proj3, proj3, proj3, bm_tab)


def _out_kernel(x_ref, ya_ref, yb_ref, p_ref, wo_ref, wg_ref, wp_ref, o_ref):
    y = (jnp.dot(ya_ref[...], wo_ref[:W_A, :], preferred_element_type=_F32)
         + jnp.dot(yb_ref[...], wo_ref[W_A:, :], preferred_element_type=_F32))
    x1 = x_ref[...] + y
    gate = jax.nn.sigmoid(jnp.dot(x1.astype(_BF16), wg_ref[...], preferred_element_type=_F32))
    ple = jnp.dot(p_ref[...].astype(_BF16), wp_ref[...], preferred_element_type=_F32)
    o_ref[...] = x1 + gate * ple


def _out(x2, ya2, yb2, p2, wo, wg, wp):
    n = x2.shape[0]
    tile = lambda w: pl.BlockSpec((TM_OUT, w), lambda i: (i, 0))
    whole = lambda a: pl.BlockSpec(a.shape, lambda i: (0, 0))
    return pl.pallas_call(
        _out_kernel,
        grid=(n // TM_OUT,),
        in_specs=[tile(D_MODEL), tile(W_A), tile(W_B), tile(PLE_DIM),
                  whole(wo), whole(wg), whole(wp)],
        out_specs=tile(D_MODEL),
        out_shape=jax.ShapeDtypeStruct((n, D_MODEL), _F32),
        compiler_params=pltpu.CompilerParams(
            dimension_semantics=("arbitrary",), vmem_limit_bytes=VMEM_LIMIT),
        name="out_proj",
    )(x2, ya2, yb2, p2, wo, wg, wp)


def _segment_ones():
    idx = np.arange(GROUP_W // 2) // HEAD_DIM
    return jnp.asarray((idx[:, None] == idx[None, :]).astype(np.float32), dtype=_BF16)


def kernel(x, p, norm_g, w_in, w_out, q_norm_a, k_norm_a, lam_q1, lam_k1, lam_q2, lam_k2,
           subln_g, t5_bias, q_norm_b, k_norm_b, na_rpb, w_ple_gate, w_ple_proj):
    b, seq, _ = x.shape
    depth = w_in.shape[0]
    n = b * seq
    rows = seq // GRID_W
    kr = min(WIN_R, rows)
    assert kr == WIN_R and rows % NA_ROWS == 0 and seq % TQ == 0 and n % TM_PROJ == 0

    seg = _segment_ones()
    bias_tab = _t5_table(t5_bias.astype(_F32), seq)
    ones_w = jnp.ones((1, GROUP_W), _F32)
    for i in range(depth):
        lam_init = 0.8 - 0.6 * math.exp(-0.3 * i)
        tile_a = lambda g, s: jnp.tile(g.astype(_F32), GROUP_W // DH_DIFF)[None, :] * s
        gains = jnp.concatenate([
            tile_a(q_norm_a[i], DH_DIFF ** -0.5 * LOG2E), tile_a(k_norm_a[i], 1.0), ones_w, ones_w,
            tile_a(q_norm_b[i], DH_NA ** -0.5 * LOG2E), tile_a(k_norm_b[i], 1.0), ones_w, ones_w],
            axis=-1)
        proj = _proj(x.reshape(n, D_MODEL), norm_g[i][None, :].astype(_F32),
                     w_in[i].astype(_BF16), seg, gains)
        proj3 = proj.reshape(b, seq, D_IN)
        lam_vecs = jnp.stack([lam_q1[i], lam_k1[i], lam_q2[i], lam_k2[i]]).astype(_F32)
        ya = _diff_attention(proj3, lam_vecs, bias_tab, subln_g[i][None, :].astype(_F32), lam_init)
        bm_tab = _na_table(na_rpb[i].astype(_F32), rows, kr)
        yb = _nbr_attention(proj3, bm_tab, kr)
        x = _out(x.reshape(n, D_MODEL), ya.reshape(n, W_A), yb.reshape(n, W_B),
                 p[i].reshape(n, PLE_DIM), w_out[i].astype(_BF16),
                 w_ple_gate[i].astype(_BF16), w_ple_proj[i].astype(_BF16)).reshape(b, seq, D_MODEL)
    return x
```

```python
import math

import jax
import jax.numpy as jnp
import numpy as np
from jax import lax
from jax.experimental import pallas as pl
from jax.experimental.pallas import tpu as pltpu

D_MODEL = 1024
PLE_DIM = 256
GRID_W = 64
H_DIFF = 4
DH_DIFF = 64
DV_DIFF = 2 * DH_DIFF
H_NA = 8
DH_NA = 64
WIN_R = 8
WIN_C = 16
N_BUCKETS = 32
MAX_DIST = 128
W_A = H_DIFF * DV_DIFF
W_B = H_NA * DH_NA
D_IN = 4 * W_A + 4 * W_B
EPS = 1e-6
NEG = -1e30

GROUP_W = 512
NORM_GROUPS = (0, 1, 4, 5)
GATE_GROUPS = (3, 7)
HEAD_DIM = 64

TM_PROJ = 512
TM_OUT = 512
TQ = 256
KEY_BLK = 128
ROW_CHUNK = 32
NA_ROWS = 4
NA_KROWS = NA_ROWS + WIN_R
LOG2E = math.log2(math.e)
VMEM_LIMIT = 56 * 1024 * 1024

_F32 = jnp.float32
_BF16 = jnp.bfloat16


def _t5_thresholds():
    half = N_BUCKETS // 2
    max_exact = half // 2
    steps = half - max_exact
    ratio = MAX_DIST // max_exact
    thr = []
    for j in range(1, steps):
        n = max_exact
        while n ** steps < (max_exact ** steps) * (ratio ** j):
            n += 1
        thr.append(n)
    return tuple(thr)


def _t5_table_rows(seq):
    return 2 * seq - KEY_BLK


def _t5_table_kernel(tb_ref, o_ref, *, seq):
    h = pl.program_id(0)
    half = N_BUCKETS // 2
    max_exact = half // 2
    blk = (KEY_BLK, KEY_BLK)
    far_pos = tb_ref[(N_BUCKETS - 1) * H_DIFF + h] * LOG2E
    far_neg = tb_ref[(half - 1) * H_DIFF + h] * LOG2E
    for rb in range(_t5_table_rows(seq) // KEY_BLK):
        d_lo = KEY_BLK * rb - (seq - KEY_BLK)
        if -(d_lo + KEY_BLK - 1) >= MAX_DIST:
            val = jnp.full(blk, far_pos, _F32)
        elif (KEY_BLK - 1) - d_lo <= -MAX_DIST:
            val = jnp.full(blk, far_neg, _F32)
        else:
            d = lax.broadcasted_iota(jnp.int32, blk, 0) + d_lo
            rel = lax.broadcasted_iota(jnp.int32, blk, 1) - d
            n = jnp.abs(rel)
            large = jnp.full_like(n, max_exact)
            for thr in _t5_thresholds():
                large = large + (n >= thr).astype(jnp.int32)
            bucket = jnp.where(rel > 0, half, 0) + jnp.where(n < max_exact, n, large)
            val = jnp.zeros(blk, _F32)
            for j in range(N_BUCKETS):
                val = jnp.where(bucket == j, tb_ref[j * H_DIFF + h] * LOG2E, val)
        o_ref[rb * KEY_BLK:(rb + 1) * KEY_BLK, :] = val.astype(_BF16)


def _t5_table(t5_bias, seq):
    rows = _t5_table_rows(seq)
    return pl.pallas_call(
        lambda tb, o: _t5_table_kernel(tb, o, seq=seq),
        grid=(H_DIFF,),
        in_specs=[pl.BlockSpec(memory_space=pltpu.SMEM)],
        out_specs=pl.BlockSpec((None, rows, KEY_BLK), lambda h: (h, 0, 0)),
        out_shape=jax.ShapeDtypeStruct((H_DIFF, rows, KEY_BLK), _BF16),
        name="t5_table",
    )(t5_bias.reshape(-1))


def _na_first_key_row(r, rows, kr):
    return min(max(r - kr // 2, 0), rows - kr)


def _na_window_start(g, rows, kr):
    return min(max(g * NA_ROWS - kr // 2, 0), rows - NA_KROWS)


def _na_variant_groups(rows):
    return (0, 1, rows // NA_ROWS - 1)


def _na_table_kernel(rpb_ref, o_ref, *, rows, kr):
    h = pl.program_id(0)
    n_dr = 2 * WIN_R - 1
    n_dc = 2 * WIN_C - 1
    pair = (GRID_W, 2 * GRID_W)
    cq = lax.broadcasted_iota(jnp.int32, pair, 0)
    lane = lax.broadcasted_iota(jnp.int32, pair, 1)
    ck = lane & (GRID_W - 1)
    dc = jnp.clip(ck - cq, -(WIN_C - 1), WIN_C - 1) + (WIN_C - 1)
    cs = jnp.clip(cq - WIN_C // 2, 0, GRID_W - WIN_C)
    col_ok = (ck >= cs) & (ck < cs + WIN_C)
    left = lane < GRID_W
    neg = jnp.full(pair, NEG, _F32)
    tiles = {}

    def tile(dr):
        if dr not in tiles:
            acc = jnp.zeros(pair, _F32)
            for j in range(n_dc):
                acc = jnp.where(dc == j, rpb_ref[(h * n_dr + dr) * n_dc + j] * LOG2E, acc)
            tiles[dr] = jnp.where(col_ok, acc, NEG)
        return tiles[dr]

    for v, g in enumerate(_na_variant_groups(rows)):
        ws = _na_window_start(g, rows, kr)
        for rr in range(NA_ROWS):
            r = g * NA_ROWS + rr
            rs = _na_first_key_row(r, rows, kr)
            for kp in range(NA_KROWS // 2):
                parts = []
                for ko in (2 * kp, 2 * kp + 1):
                    in_win = 0 <= ws + ko - rs < kr
                    parts.append(tile(ws + ko - r + WIN_R - 1) if in_win else neg)
                val = parts[0] if parts[0] is parts[1] else jnp.where(left, parts[0], parts[1])
                o_ref[v, rr * GRID_W:(rr + 1) * GRID_W, kp * 2 * GRID_W:(kp + 1) * 2 * GRID_W] = val


def _na_table(rpb, rows, kr):
    n_var = len(_na_variant_groups(rows))
    shape = (n_var, H_NA, NA_ROWS * GRID_W, NA_KROWS * GRID_W)
    return pl.pallas_call(
        lambda r, o: _na_table_kernel(r, o, rows=rows, kr=kr),
        grid=(H_NA,),
        in_specs=[pl.BlockSpec(memory_space=pltpu.SMEM)],
        out_specs=pl.BlockSpec((n_var, None) + shape[2:], lambda h: (0, h, 0, 0)),
        out_shape=jax.ShapeDtypeStruct(shape, _F32),
        name="na_table",
    )(rpb.reshape(-1))


def _proj_kernel(x_ref, ng_ref, w_ref, seg_ref, gain_ref, o_ref):
    x = x_ref[...]
    ms = jnp.mean(x * x, axis=-1, keepdims=True)
    xn = (x * lax.rsqrt(ms + EPS) * ng_ref[...]).astype(_BF16)
    seg = seg_ref[...]
    half_w = GROUP_W // 2
    for j in range(D_IN // GROUP_W):
        sl = slice(j * GROUP_W, (j + 1) * GROUP_W)
        acc = jnp.dot(xn, w_ref[:, sl], preferred_element_type=_F32)
        if j in NORM_GROUPS:
            sq = (acc * acc).astype(_BF16)
            ss = jnp.concatenate(
                [jnp.dot(sq[:, :half_w], seg, preferred_element_type=_F32),
                 jnp.dot(sq[:, half_w:], seg, preferred_element_type=_F32)], axis=-1)
            acc = acc * lax.rsqrt(ss * (1.0 / HEAD_DIM) + EPS) * gain_ref[:, sl]
        elif j in GATE_GROUPS:
            acc = acc * jax.nn.sigmoid(acc)
        o_ref[:, sl] = acc.astype(_BF16)


def _proj(x2, norm_g, w_bf, seg, gains):
    n = x2.shape[0]
    return pl.pallas_call(
        _proj_kernel,
        grid=(n // TM_PROJ,),
        in_specs=[
            pl.BlockSpec((TM_PROJ, D_MODEL), lambda i: (i, 0)),
            pl.BlockSpec((1, D_MODEL), lambda i: (0, 0)),
            pl.BlockSpec((D_MODEL, D_IN), lambda i: (0, 0)),
            pl.BlockSpec(seg.shape, lambda i: (0, 0)),
            pl.BlockSpec((1, D_IN), lambda i: (0, 0)),
        ],
        out_specs=pl.BlockSpec((TM_PROJ, D_IN), lambda i: (i, 0)),
        out_shape=jax.ShapeDtypeStruct((n, D_IN), _BF16),
        compiler_params=pltpu.CompilerParams(
            dimension_semantics=("arbitrary",), vmem_limit_bytes=VMEM_LIMIT),
        name="proj",
    )(x2, norm_g, w_bf, seg, gains)


def _diff_kernel(lam_ref, q_ref, k_ref, v_ref, z_ref, tab_ref, g_ref, o_ref,
                 kaug_scr, s_even, s_odd, l_scr, e1_scr, e2_scr,
                 *, seq, n_steps, lam_init):
    nq = seq // TQ
    nkb = seq // KEY_BLK
    t = pl.program_id(0)
    t_a = jnp.minimum(t, n_steps - 1)
    i_a = t_a % nq
    h_a = (t_a // nq) % H_DIFF
    nt = (((1,), (1,)), ((), ()))

    @pl.when(t == 0)
    def _():
        row = lax.broadcasted_iota(jnp.int32, (KEY_BLK, KEY_BLK), 0)
        col = lax.broadcasted_iota(jnp.int32, (KEY_BLK, KEY_BLK), 1)
        eye = (row == col).astype(_BF16)
        for j in range(2 * nkb):
            kaug_scr[j * KEY_BLK:(j + 1) * KEY_BLK, KEY_BLK:] = eye
        s_odd[...] = jnp.zeros_like(s_odd)

    @pl.when(i_a == 0)
    def _():
        lane = lax.broadcasted_iota(jnp.int32, (KEY_BLK, DV_DIFF), 1)
        zero = jnp.zeros((KEY_BLK, DV_DIFF), _BF16)
        for j in range(nkb):
            kj = k_ref[j * KEY_BLK:(j + 1) * KEY_BLK, :]
            kaug_scr[2 * j * KEY_BLK:(2 * j + 1) * KEY_BLK, :KEY_BLK] = jnp.where(lane < DH_DIFF, kj, zero)
            kaug_scr[(2 * j + 1) * KEY_BLK:(2 * j + 2) * KEY_BLK, :KEY_BLK] = jnp.where(lane >= DH_DIFF, kj, zero)

    def score_stage(s_w):
        q = q_ref[...]
        for j in range(nkb):
            off = pl.multiple_of(i_a * TQ + (seq - KEY_BLK - KEY_BLK * j), KEY_BLK)
            lhs = jnp.concatenate([q, tab_ref[h_a, pl.ds(off, TQ), :]], axis=1)
            w = kaug_scr[2 * j * KEY_BLK:(2 * j + 2) * KEY_BLK, :]
            s_w[:, 2 * j * KEY_BLK:(2 * j + 2) * KEY_BLK] = lax.dot_general(
                lhs, w, nt, preferred_element_type=_F32)

    def softmax_pv_stage(s_r):
        lv = lam_ref[...]
        lam = (jnp.exp(jnp.sum(lv[0:1] * lv[1:2], axis=-1, keepdims=True))
               - jnp.exp(jnp.sum(lv[2:3] * lv[3:4], axis=-1, keepdims=True)) + lam_init)
        for rc in range(TQ // ROW_CHUNK):
            rows = slice(rc * ROW_CHUNK, (rc + 1) * ROW_CHUNK)
            tile = lambda mp, j: s_r[rows, (2 * j + mp) * KEY_BLK:(2 * j + mp + 1) * KEY_BLK]
            mx1, mx2 = tile(0, 0), tile(1, 0)
            for j in range(1, nkb):
                mx1 = jnp.maximum(mx1, tile(0, j))
                mx2 = jnp.maximum(mx2, tile(1, j))
            m1 = jnp.broadcast_to(jnp.max(mx1, axis=-1, keepdims=True), (ROW_CHUNK, KEY_BLK))
            m2 = jnp.broadcast_to(jnp.max(mx2, axis=-1, keepdims=True), (ROW_CHUNK, KEY_BLK))
            l1 = jnp.zeros((ROW_CHUNK, KEY_BLK), _F32)
            l2 = jnp.zeros((ROW_CHUNK, KEY_BLK), _F32)
            for j in range(nkb):
                e1 = jnp.exp2(tile(0, j) - m1)
                e2 = jnp.exp2(tile(1, j) - m2)
                l1 = l1 + e1
                l2 = l2 + e2
                e1_scr[rows, j * KEY_BLK:(j + 1) * KEY_BLK] = e1.astype(_BF16)
                e2_scr[rows, j * KEY_BLK:(j + 1) * KEY_BLK] = e2.astype(_BF16)
            l_scr[0, rows, :] = jnp.sum(l1, axis=-1, keepdims=True)
            l_scr[1, rows, :] = jnp.sum(l2, axis=-1, keepdims=True)
        l1 = l_scr[0]
        l2 = l_scr[1]
        r = jnp.broadcast_to(lam * l1 / l2, (TQ, 2 * KEY_BLK)).astype(_BF16)
        acc = jnp.zeros((TQ, DV_DIFF), _F32)
        for jj in range(seq // (2 * KEY_BLK)):
            sl = slice(jj * 2 * KEY_BLK, (jj + 1) * 2 * KEY_BLK)
            a = e1_scr[:, sl] - r * e2_scr[:, sl]
            acc = acc + jnp.dot(a, v_ref[sl, :], preferred_element_type=_F32)
        o = acc * (1.0 / l1)
        ms = jnp.mean(o * o, axis=-1, keepdims=True)
        o = o * lax.rsqrt(ms + EPS) * g_ref[...] * (1.0 - lam_init)
        o_ref[...] = (o * z_ref[...].astype(_F32)).astype(_BF16)

    @pl.when(t % 2 == 0)
    def _():
        score_stage(s_even)
        softmax_pv_stage(s_odd)

    @pl.when(t % 2 == 1)
    def _():
        score_stage(s_odd)
        softmax_pv_stage(s_even)


def _diff_attention(proj3, lam_vecs, tab, subln_g, lam_init):
    b, seq, _ = proj3.shape
    nq = seq // TQ
    n_steps = b * H_DIFF * nq

    def split(t):
        return t // (H_DIFF * nq), (t // nq) % H_DIFF, t % nq

    def cur(col0, whole):
        def index(t):
            bi, h, i = split(jnp.minimum(t, n_steps - 1))
            return bi, (0 if whole else i), col0 + h
        return index

    def prev(col0, whole):
        def index(t):
            bi, h, i = split(jnp.maximum(t - 1, 0))
            return bi, (0 if whole else i), col0 + h
        return index

    s_shape = pltpu.VMEM((TQ, 2 * seq), _F32)
    return pl.pallas_call(
        lambda *refs: _diff_kernel(*refs, seq=seq, n_steps=n_steps, lam_init=lam_init),
        grid=(n_steps + 1,),
        in_specs=[
            pl.BlockSpec(lam_vecs.shape, lambda t: (0, 0)),
            pl.BlockSpec((None, TQ, DV_DIFF), cur(0, False)),
            pl.BlockSpec((None, seq, DV_DIFF), cur(H_DIFF, True)),
            pl.BlockSpec((None, seq, DV_DIFF), prev(2 * H_DIFF, True)),
            pl.BlockSpec((None, TQ, DV_DIFF), prev(3 * H_DIFF, False)),
            pl.BlockSpec(tab.shape, lambda t: (0, 0, 0)),
            pl.BlockSpec((1, DV_DIFF), lambda t: (0, 0)),
        ],
        out_specs=pl.BlockSpec((None, TQ, DV_DIFF), prev(0, False)),
        out_shape=jax.ShapeDtypeStruct((b, seq, W_A), _BF16),
        scratch_shapes=[
            pltpu.VMEM((2 * seq, 2 * KEY_BLK), _BF16),
            s_shape, s_shape, pltpu.VMEM((2, TQ, 1), _F32),
            pltpu.VMEM((TQ, seq), _BF16), pltpu.VMEM((TQ, seq), _BF16),
        ],
        compiler_params=pltpu.CompilerParams(
            dimension_semantics=("arbitrary",), vmem_limit_bytes=VMEM_LIMIT),
        name="diff_attn",
    )(lam_vecs, proj3, proj3, proj3, proj3, tab, subln_g)


def _nbr_kernel(q_ref, k_ref, v_ref, z_ref, bm_ref, o_ref, ke_scr, ko_scr, *, rows, kr):
    g = pl.program_id(1)
    pair_w = 2 * DH_NA

    @pl.when(g == 0)
    def _():
        k = k_ref[...]
        lane = lax.broadcasted_iota(jnp.int32, k.shape, 1) & (pair_w - 1)
        zero = jnp.zeros_like(k)
        ke_scr[...] = jnp.where(lane < DH_NA, k, zero)
        ko_scr[...] = jnp.where(lane >= DH_NA, k, zero)

    nt = (((1,), (1,)), ((), ()))
    win = NA_KROWS * GRID_W
    tok = NA_ROWS * GRID_W
    ws = jnp.clip(g * NA_ROWS - kr // 2, 0, rows - NA_KROWS)
    start = pl.multiple_of(ws * GRID_W, GRID_W)
    out_lane = lax.broadcasted_iota(jnp.int32, (tok, pair_w), 1)
    for hp in range(H_NA // 2):
        csl = slice(hp * pair_w, (hp + 1) * pair_w)
        qp = q_ref[:, csl]
        vw = v_ref[pl.ds(start, win), csl]
        halves = []
        for par, kscr in ((0, ke_scr), (1, ko_scr)):
            kw = kscr[pl.ds(start, win), csl]
            s = lax.dot_general(qp, kw, nt, preferred_element_type=_F32)
            s = s + bm_ref[2 * hp + par]
            e = jnp.exp2(s - jnp.max(s, axis=-1, keepdims=True))
            inv = 1.0 / jnp.sum(e, axis=-1, keepdims=True)
            halves.append(jnp.dot(e.astype(_BF16), vw, preferred_element_type=_F32) * inv)
        o = jnp.where(out_lane < DH_NA, halves[0], halves[1])
        o_ref[:, csl] = (o * z_ref[:, csl].astype(_F32)).astype(_BF16)


def _nbr_attention(proj3, bm_tab, kr):
    b, seq, _ = proj3.shape
    rows = seq // GRID_W
    n_grp = rows // NA_ROWS
    tok = NA_ROWS * GRID_W
    sig = lambda g: tuple(
        (0 <= _na_window_start(g, rows, kr) + ko - _na_first_key_row(g * NA_ROWS + rr, rows, kr) < kr,
         _na_window_start(g, rows, kr) + ko - (g * NA_ROWS + rr))
        for rr in range(NA_ROWS) for ko in range(NA_KROWS))
    variant_of = lambda g: (g > 0) + (g == n_grp - 1)
    reps = _na_variant_groups(rows)
    assert all(sig(g) == sig(reps[variant_of(g)]) for g in range(n_grp))
    grp = lambda gi: (lambda bi, g: (bi, g, gi))
    full = lambda gi: (lambda bi, g: (bi, 0, gi))
    return pl.pallas_call(
        lambda *refs: _nbr_kernel(*refs, rows=rows, kr=kr),
        grid=(b, n_grp),
        in_specs=[
            pl.BlockSpec((None, tok, W_B), grp(4)),
            pl.BlockSpec((None, seq, W_B), full(5)),
            pl.BlockSpec((None, seq, W_B), full(6)),
            pl.BlockSpec((None, tok, W_B), grp(7)),
            pl.BlockSpec((None,) + bm_tab.shape[1:],
                         lambda bi, g: ((g > 0).astype(jnp.int32) + (g == n_grp - 1).astype(jnp.int32),
                                        0, 0, 0)),
        ],
        out_specs=pl.BlockSpec((None, tok, W_B), lambda bi, g: (bi, g, 0)),
        out_shape=jax.ShapeDtypeStruct((b, seq, W_B), _BF16),
        scratch_shapes=[pltpu.VMEM((seq, W_B), _BF16), pltpu.VMEM((seq, W_B), _BF16)],
        compiler_params=pltpu.CompilerParams(
            dimension_semantics=("arbitrary", "arbitrary"), vmem_limit_bytes=VMEM_LIMIT),
        name="nbr_attn",
    )(proj3, proj3, proj3, proj3, bm_tab)


def _out_kernel(x_ref, ya_ref, yb_ref, p_ref, wo_ref, wg_ref, wp_ref, o_ref):
    y = (jnp.dot(ya_ref[...], wo_ref[:W_A, :], preferred_element_type=_F32)
         + jnp.dot(yb_ref[...], wo_ref[W_A:, :], preferred_element_type=_F32))
    x1 = x_ref[...] + y
    gate = jax.nn.sigmoid(jnp.dot(x1.astype(_BF16), wg_ref[...], preferred_element_type=_F32))
    ple = jnp.dot(p_ref[...].astype(_BF16), wp_ref[...], preferred_element_type=_F32)
    o_ref[...] = x1 + gate * ple


def _out(x2, ya2, yb2, p2, wo, wg, wp):
    n = x2.shape[0]
    tile = lambda w: pl.BlockSpec((TM_OUT, w), lambda i: (i, 0))
    whole = lambda a: pl.BlockSpec(a.shape, lambda i: (0, 0))
    return pl.pallas_call(
        _out_kernel,
        grid=(n // TM_OUT,),
        in_specs=[tile(D_MODEL), tile(W_A), tile(W_B), tile(PLE_DIM),
                  whole(wo), whole(wg), whole(wp)],
        out_specs=tile(D_MODEL),
        out_shape=jax.ShapeDtypeStruct((n, D_MODEL), _F32),
        compiler_params=pltpu.CompilerParams(
            dimension_semantics=("arbitrary",), vmem_limit_bytes=VMEM_LIMIT),
        name="out_proj",
    )(x2, ya2, yb2, p2, wo, wg, wp)


def _segment_ones():
    idx = np.arange(GROUP_W // 2) // HEAD_DIM
    return jnp.asarray((idx[:, None] == idx[None, :]).astype(np.float32), dtype=_BF16)


def kernel(x, p, norm_g, w_in, w_out, q_norm_a, k_norm_a, lam_q1, lam_k1, lam_q2, lam_k2,
           subln_g, t5_bias, q_norm_b, k_norm_b, na_rpb, w_ple_gate, w_ple_proj):
    b, seq, _ = x.shape
    depth = w_in.shape[0]
    n = b * seq
    rows = seq // GRID_W
    kr = min(WIN_R, rows)
    assert kr == WIN_R and rows % NA_ROWS == 0 and seq % TQ == 0 and n % TM_PROJ == 0

    seg = _segment_ones()
    bias_tab = _t5_table(t5_bias.astype(_F32), seq)
    ones_w = jnp.ones((1, GROUP_W), _F32)
    for i in range(depth):
        lam_init = 0.8 - 0.6 * math.exp(-0.3 * i)
        tile_a = lambda g, s: jnp.tile(g.astype(_F32), GROUP_W // DH_DIFF)[None, :] * s
        gains = jnp.concatenate([
            tile_a(q_norm_a[i], DH_DIFF ** -0.5 * LOG2E), tile_a(k_norm_a[i], 1.0), ones_w, ones_w,
            tile_a(q_norm_b[i], DH_NA ** -0.5 * LOG2E), tile_a(k_norm_b[i], 1.0), ones_w, ones_w],
            axis=-1)
        proj = _proj(x.reshape(n, D_MODEL), norm_g[i][None, :].astype(_F32),
                     w_in[i].astype(_BF16), seg, gains)
        proj3 = proj.reshape(b, seq, D_IN)
        lam_vecs = jnp.stack([lam_q1[i], lam_k1[i], lam_q2[i], lam_k2[i]]).astype(_F32)
        ya = _diff_attention(proj3, lam_vecs, bias_tab, subln_g[i][None, :].astype(_F32), lam_init)
        bm_tab = _na_table(na_rpb[i].astype(_F32), rows, kr)
        yb = _nbr_attention(proj3, bm_tab, kr)
        x = _out(x.reshape(n, D_MODEL), ya.reshape(n, W_A), yb.reshape(n, W_B),
                 p[i].reshape(n, PLE_DIM), w_out[i].astype(_BF16),
                 w_ple_gate[i].astype(_BF16), w_ple_proj[i].astype(_BF16)).reshape(b, seq, D_MODEL)
    return x
```

```python
import math

import jax
import jax.numpy as jnp
import numpy as np
from jax import lax
from jax.experimental import pallas as pl
from jax.experimental.pallas import tpu as pltpu

D_MODEL = 1024
PLE_DIM = 256
GRID_W = 64
H_DIFF = 4
DH_DIFF = 64
DV_DIFF = 2 * DH_DIFF
H_NA = 8
DH_NA = 64
WIN_R = 8
WIN_C = 16
N_BUCKETS = 32
MAX_DIST = 128
W_A = H_DIFF * DV_DIFF
W_B = H_NA * DH_NA
D_IN = 4 * W_A + 4 * W_B
EPS = 1e-6
NEG = -1e30

GROUP_W = 512
NORM_GROUPS = (0, 1, 4, 5)
GATE_GROUPS = (3, 7)
HEAD_DIM = 64

TM_PROJ = 512
TM_OUT = 512
TQ = 256
KEY_BLK = 128
ROW_CHUNK = 32
NA_ROWS = 4
NA_KROWS = NA_ROWS + WIN_R
LOG2E = math.log2(math.e)
VMEM_LIMIT = 56 * 1024 * 1024

_F32 = jnp.float32
_BF16 = jnp.bfloat16


def _t5_thresholds():
    half = N_BUCKETS // 2
    max_exact = half // 2
    steps = half - max_exact
    ratio = MAX_DIST // max_exact
    thr = []
    for j in range(1, steps):
        n = max_exact
        while n ** steps < (max_exact ** steps) * (ratio ** j):
            n += 1
        thr.append(n)
    return tuple(thr)


def _t5_table_rows(seq):
    return 2 * seq - KEY_BLK


def _t5_table_kernel(tb_ref, o_ref, *, seq):
    h = pl.program_id(0)
    half = N_BUCKETS // 2
    max_exact = half // 2
    blk = (KEY_BLK, KEY_BLK)
    far_pos = tb_ref[(N_BUCKETS - 1) * H_DIFF + h] * LOG2E
    far_neg = tb_ref[(half - 1) * H_DIFF + h] * LOG2E
    for rb in range(_t5_table_rows(seq) // KEY_BLK):
        d_lo = KEY_BLK * rb - (seq - KEY_BLK)
        if -(d_lo + KEY_BLK - 1) >= MAX_DIST:
            val = jnp.full(blk, far_pos, _F32)
        elif (KEY_BLK - 1) - d_lo <= -MAX_DIST:
            val = jnp.full(blk, far_neg, _F32)
        else:
            d = lax.broadcasted_iota(jnp.int32, blk, 0) + d_lo
            rel = lax.broadcasted_iota(jnp.int32, blk, 1) - d
            n = jnp.abs(rel)
            large = jnp.full_like(n, max_exact)
            for thr in _t5_thresholds():
                large = large + (n >= thr).astype(jnp.int32)
            bucket = jnp.where(rel > 0, half, 0) + jnp.where(n < max_exact, n, large)
            val = jnp.zeros(blk, _F32)
            for j in range(N_BUCKETS):
                val = jnp.where(bucket == j, tb_ref[j * H_DIFF + h] * LOG2E, val)
        o_ref[rb * KEY_BLK:(rb + 1) * KEY_BLK, :] = val.astype(_BF16)


def _t5_table(t5_bias, seq):
    rows = _t5_table_rows(seq)
    return pl.pallas_call(
        lambda tb, o: _t5_table_kernel(tb, o, seq=seq),
        grid=(H_DIFF,),
        in_specs=[pl.BlockSpec(memory_space=pltpu.SMEM)],
        out_specs=pl.BlockSpec((None, rows, KEY_BLK), lambda h: (h, 0, 0)),
        out_shape=jax.ShapeDtypeStruct((H_DIFF, rows, KEY_BLK), _BF16),
        name="t5_table",
    )(t5_bias.reshape(-1))


def _na_first_key_row(r, rows, kr):
    return min(max(r - kr // 2, 0), rows - kr)


def _na_window_start(g, rows, kr):
    return min(max(g * NA_ROWS - kr // 2, 0), rows - NA_KROWS)


def _na_variant_groups(rows):
    return (0, 1, rows // NA_ROWS - 1)


def _na_table_kernel(rpb_ref, o_ref, *, rows, kr):
    h = pl.program_id(0)
    n_dr = 2 * WIN_R - 1
    n_dc = 2 * WIN_C - 1
    pair = (GRID_W, 2 * GRID_W)
    cq = lax.broadcasted_iota(jnp.int32, pair, 0)
    lane = lax.broadcasted_iota(jnp.int32, pair, 1)
    ck = lane & (GRID_W - 1)
    dc = jnp.clip(ck - cq, -(WIN_C - 1), WIN_C - 1) + (WIN_C - 1)
    cs = jnp.clip(cq - WIN_C // 2, 0, GRID_W - WIN_C)
    col_ok = (ck >= cs) & (ck < cs + WIN_C)
    left = lane < GRID_W
    neg = jnp.full(pair, NEG, _F32)
    tiles = {}

    def tile(dr):
        if dr not in tiles:
            acc = jnp.zeros(pair, _F32)
            for j in range(n_dc):
                acc = jnp.where(dc == j, rpb_ref[(h * n_dr + dr) * n_dc + j] * LOG2E, acc)
            tiles[dr] = jnp.where(col_ok, acc, NEG)
        return tiles[dr]

    for v, g in enumerate(_na_variant_groups(rows)):
        ws = _na_window_start(g, rows, kr)
        for rr in range(NA_ROWS):
            r = g * NA_ROWS + rr
            rs = _na_first_key_row(r, rows, kr)
            for kp in range(NA_KROWS // 2):
                parts = []
                for ko in (2 * kp, 2 * kp + 1):
                    in_win = 0 <= ws + ko - rs < kr
                    parts.append(tile(ws + ko - r + WIN_R - 1) if in_win else neg)
                val = parts[0] if parts[0] is parts[1] else jnp.where(left, parts[0], parts[1])
                o_ref[v, rr * GRID_W:(rr + 1) * GRID_W, kp * 2 * GRID_W:(kp + 1) * 2 * GRID_W] = val


def _na_table(rpb, rows, kr):
    n_var = len(_na_variant_groups(rows))
    shape = (n_var, H_NA, NA_ROWS * GRID_W, NA_KROWS * GRID_W)
    return pl.pallas_call(
        lambda r, o: _na_table_kernel(r, o, rows=rows, kr=kr),
        grid=(H_NA,),
        in_specs=[pl.BlockSpec(memory_space=pltpu.SMEM)],
        out_specs=pl.BlockSpec((n_var, None) + shape[2:], lambda h: (0, h, 0, 0)),
        out_shape=jax.ShapeDtypeStruct(shape, _F32),
        name="na_table",
    )(rpb.reshape(-1))


def _proj_kernel(x_ref, ng_ref, w_ref, seg_ref, gain_ref, o_ref):
    x = x_ref[...]
    ms = jnp.mean(x * x, axis=-1, keepdims=True)
    xn = (x * lax.rsqrt(ms + EPS) * ng_ref[...]).astype(_BF16)
    seg = seg_ref[...]
    half_w = GROUP_W // 2
    for j in range(D_IN // GROUP_W):
        sl = slice(j * GROUP_W, (j + 1) * GROUP_W)
        acc = jnp.dot(xn, w_ref[:, sl], preferred_element_type=_F32)
        if j in NORM_GROUPS:
            sq = (acc * acc).astype(_BF16)
            ss = jnp.concatenate(
                [jnp.dot(sq[:, :half_w], seg, preferred_element_type=_F32),
                 jnp.dot(sq[:, half_w:], seg, preferred_element_type=_F32)], axis=-1)
            acc = acc * lax.rsqrt(ss * (1.0 / HEAD_DIM) + EPS) * gain_ref[:, sl]
        elif j in GATE_GROUPS:
            acc = acc * jax.nn.sigmoid(acc)
        o_ref[:, sl] = acc.astype(_BF16)


def _proj(x2, norm_g, w_bf, seg, gains):
    n = x2.shape[0]
    return pl.pallas_call(
        _proj_kernel,
        grid=(n // TM_PROJ,),
        in_specs=[
            pl.BlockSpec((TM_PROJ, D_MODEL), lambda i: (i, 0)),
            pl.BlockSpec((1, D_MODEL), lambda i: (0, 0)),
            pl.BlockSpec((D_MODEL, D_IN), lambda i: (0, 0)),
            pl.BlockSpec(seg.shape, lambda i: (0, 0)),
            pl.BlockSpec((1, D_IN), lambda i: (0, 0)),
        ],
        out_specs=pl.BlockSpec((TM_PROJ, D_IN), lambda i: (i, 0)),
        out_shape=jax.ShapeDtypeStruct((n, D_IN), _BF16),
        compiler_params=pltpu.CompilerParams(
            dimension_semantics=("arbitrary",), vmem_limit_bytes=VMEM_LIMIT),
        name="proj",
    )(x2, norm_g, w_bf, seg, gains)


def _diff_kernel(lam_ref, q_ref, k_ref, v_ref, z_ref, tab_ref, g_ref, o_ref,
                 kaug_scr, s_even, s_odd, l_even, l_odd, e1_even, e2_even, e1_odd, e2_odd,
                 *, seq, n_steps, lam_init):
    nq = seq // TQ
    nkb = seq // KEY_BLK
    t = pl.program_id(0)
    t_a = jnp.minimum(t, n_steps - 1)
    i_a = t_a % nq
    h_a = (t_a // nq) % H_DIFF
    nt = (((1,), (1,)), ((), ()))

    @pl.when(t == 0)
    def _():
        row = lax.broadcasted_iota(jnp.int32, (KEY_BLK, KEY_BLK), 0)
        col = lax.broadcasted_iota(jnp.int32, (KEY_BLK, KEY_BLK), 1)
        eye = (row == col).astype(_BF16)
        for j in range(2 * nkb):
            kaug_scr[j * KEY_BLK:(j + 1) * KEY_BLK, KEY_BLK:] = eye
        s_odd[...] = jnp.zeros_like(s_odd)
        e1_even[...] = jnp.zeros_like(e1_even)
        e2_even[...] = jnp.zeros_like(e2_even)
        l_even[...] = jnp.ones_like(l_even)

    @pl.when(i_a == 0)
    def _():
        lane = lax.broadcasted_iota(jnp.int32, (KEY_BLK, DV_DIFF), 1)
        zero = jnp.zeros((KEY_BLK, DV_DIFF), _BF16)
        for j in range(nkb):
            kj = k_ref[j * KEY_BLK:(j + 1) * KEY_BLK, :]
            kaug_scr[2 * j * KEY_BLK:(2 * j + 1) * KEY_BLK, :KEY_BLK] = jnp.where(lane < DH_DIFF, kj, zero)
            kaug_scr[(2 * j + 1) * KEY_BLK:(2 * j + 2) * KEY_BLK, :KEY_BLK] = jnp.where(lane >= DH_DIFF, kj, zero)

    def stages(s_w, s_r, e1_w, e2_w, l_w, e1_r, e2_r, l_r):
        q = q_ref[...]

        def score_block(j):
            off = pl.multiple_of(i_a * TQ + (seq - KEY_BLK - KEY_BLK * j), KEY_BLK)
            lhs = jnp.concatenate([q, tab_ref[h_a, pl.ds(off, TQ), :]], axis=1)
            w = kaug_scr[2 * j * KEY_BLK:(2 * j + 2) * KEY_BLK, :]
            sj = lax.dot_general(lhs, w, nt, preferred_element_type=_F32)
            s_w[:, 2 * j * KEY_BLK:(2 * j + 2) * KEY_BLK] = sj
            edge = jnp.minimum(sj[:ROW_CHUNK, :KEY_BLK], sj[TQ - ROW_CHUNK:, KEY_BLK:])
            return jnp.minimum(edge, -jnp.inf)

        def softmax_chunk(rc, anchor):
            rows = slice(rc * ROW_CHUNK, (rc + 1) * ROW_CHUNK)
            tile = lambda mp, j: s_r[rows, (2 * j + mp) * KEY_BLK:(2 * j + mp + 1) * KEY_BLK]
            mx1, mx2 = tile(0, 0), tile(1, 0)
            if anchor is not None:
                mx1 = jnp.maximum(mx1, anchor)
            for j in range(1, nkb):
                mx1 = jnp.maximum(mx1, tile(0, j))
                mx2 = jnp.maximum(mx2, tile(1, j))
            m1 = jnp.broadcast_to(jnp.max(mx1, axis=-1, keepdims=True), (ROW_CHUNK, KEY_BLK))
            m2 = jnp.broadcast_to(jnp.max(mx2, axis=-1, keepdims=True), (ROW_CHUNK, KEY_BLK))
            l1 = jnp.zeros((ROW_CHUNK, KEY_BLK), _F32)
            l2 = jnp.zeros((ROW_CHUNK, KEY_BLK), _F32)
            for j in range(nkb):
                e1 = jnp.exp2(tile(0, j) - m1)
                e2 = jnp.exp2(tile(1, j) - m2)
                l1 = l1 + e1
                l2 = l2 + e2
                e1_w[rows, j * KEY_BLK:(j + 1) * KEY_BLK] = e1.astype(_BF16)
                e2_w[rows, j * KEY_BLK:(j + 1) * KEY_BLK] = e2.astype(_BF16)
            l_w[0, rows, :] = jnp.sum(l1, axis=-1, keepdims=True)
            l_w[1, rows, :] = jnp.sum(l2, axis=-1, keepdims=True)

        n_chunks = TQ // ROW_CHUNK
        per_chunk = nkb // n_chunks
        anchor = None
        for rc in range(n_chunks):
            softmax_chunk(rc, anchor)
            anchor = None
            for j in range(rc * per_chunk, (rc + 1) * per_chunk):
                a_j = score_block(j)
                anchor = a_j if anchor is None else jnp.minimum(anchor, a_j)

        lv = lam_ref[...]
        lam = (jnp.exp(jnp.sum(lv[0:1] * lv[1:2], axis=-1, keepdims=True))
               - jnp.exp(jnp.sum(lv[2:3] * lv[3:4], axis=-1, keepdims=True)) + lam_init)
        l1 = l_r[0]
        l2 = l_r[1]
        r = jnp.broadcast_to(lam * l1 / l2, (TQ, 2 * KEY_BLK)).astype(_BF16)
        acc = jnp.zeros((TQ, DV_DIFF), _F32)
        for jj in range(seq // (2 * KEY_BLK)):
            sl = slice(jj * 2 * KEY_BLK, (jj + 1) * 2 * KEY_BLK)
            a = e1_r[:, sl] - r * e2_r[:, sl]
            acc = acc + jnp.dot(a, v_ref[sl, :], preferred_element_type=_F32)
        o = acc * (1.0 / l1)
        ms = jnp.mean(o * o, axis=-1, keepdims=True)
        o = o * lax.rsqrt(ms + EPS) * g_ref[...] * (1.0 - lam_init)
        o_ref[...] = (o * z_ref[...].astype(_F32)).astype(_BF16)

    @pl.when(t % 2 == 0)
    def _():
        stages(s_even, s_odd, e1_odd, e2_odd, l_odd, e1_even, e2_even, l_even)

    @pl.when(t % 2 == 1)
    def _():
        stages(s_odd, s_even, e1_even, e2_even, l_even, e1_odd, e2_odd, l_odd)


def _diff_attention(proj3, lam_vecs, tab, subln_g, lam_init):
    b, seq, _ = proj3.shape
    nq = seq // TQ
    n_steps = b * H_DIFF * nq

    def split(t):
        return t // (H_DIFF * nq), (t // nq) % H_DIFF, t % nq

    def cur(col0, whole):
        def index(t):
            bi, h, i = split(jnp.minimum(t, n_steps - 1))
            return bi, (0 if whole else i), col0 + h
        return index

    def prev(col0, whole):
        def index(t):
            bi, h, i = split(jnp.clip(t - 2, 0, n_steps - 1))
            return bi, (0 if whole else i), col0 + h
        return index

    e_shape = pltpu.VMEM((TQ, seq), _BF16)
    l_shape = pltpu.VMEM((2, TQ, 1), _F32)
    return pl.pallas_call(
        lambda *refs: _diff_kernel(*refs, seq=seq, n_steps=n_steps, lam_init=lam_init),
        grid=(n_steps + 2,),
        in_specs=[
            pl.BlockSpec(lam_vecs.shape, lambda t: (0, 0)),
            pl.BlockSpec((None, TQ, DV_DIFF), cur(0, False)),
            pl.BlockSpec((None, seq, DV_DIFF), cur(H_DIFF, True)),
            pl.BlockSpec((None, seq, DV_DIFF), prev(2 * H_DIFF, True)),
            pl.BlockSpec((None, TQ, DV_DIFF), prev(3 * H_DIFF, False)),
            pl.BlockSpec(tab.shape, lambda t: (0, 0, 0)),
            pl.BlockSpec((1, DV_DIFF), lambda t: (0, 0)),
        ],
        out_specs=pl.BlockSpec((None, TQ, DV_DIFF), prev(0, False)),
        out_shape=jax.ShapeDtypeStruct((b, seq, W_A), _BF16),
        scratch_shapes=[
            pltpu.VMEM((2 * seq, 2 * KEY_BLK), _BF16),
            pltpu.VMEM((TQ, 2 * seq), _F32), pltpu.VMEM((TQ, 2 * seq), _F32),
            l_shape, l_shape, e_shape, e_shape, e_shape, e_shape,
        ],
        compiler_params=pltpu.CompilerParams(
            dimension_semantics=("arbitrary",), vmem_limit_bytes=VMEM_LIMIT),
        name="diff_attn",
    )(lam_vecs, proj3, proj3, proj3, proj3, tab, subln_g)


def _nbr_kernel(q_ref, k_ref, v_ref, z_ref, bm_ref, o_ref, ke_scr, ko_scr, *, rows, kr):
    g = pl.program_id(1)
    pair_w = 2 * DH_NA

    @pl.when(g == 0)
    def _():
        k = k_ref[...]
        lane = lax.broadcasted_iota(jnp.int32, k.shape, 1) & (pair_w - 1)
        zero = jnp.zeros_like(k)
        ke_scr[...] = jnp.where(lane < DH_NA, k, zero)
        ko_scr[...] = jnp.where(lane >= DH_NA, k, zero)

    nt = (((1,), (1,)), ((), ()))
    win = NA_KROWS * GRID_W
    tok = NA_ROWS * GRID_W
    ws = jnp.clip(g * NA_ROWS - kr // 2, 0, rows - NA_KROWS)
    start = pl.multiple_of(ws * GRID_W, GRID_W)
    out_lane = lax.broadcasted_iota(jnp.int32, (tok, pair_w), 1)
    probs = []
    for h in range(H_NA):
        csl = slice((h // 2) * pair_w, (h // 2 + 1) * pair_w)
        kw = (ke_scr, ko_scr)[h % 2][pl.ds(start, win), csl]
        s = lax.dot_general(q_ref[:, csl], kw, nt, preferred_element_type=_F32)
        s = s + bm_ref[h]
        e = jnp.exp2(s - jnp.max(s, axis=-1, keepdims=True))
        probs.append((e.astype(_BF16), 1.0 / jnp.sum(e, axis=-1, keepdims=True)))
    for hp in range(H_NA // 2):
        csl = slice(hp * pair_w, (hp + 1) * pair_w)
        vw = v_ref[pl.ds(start, win), csl]
        halves = [jnp.dot(e, vw, preferred_element_type=_F32) * inv
                  for e, inv in probs[2 * hp:2 * hp + 2]]
        o = jnp.where(out_lane < DH_NA, halves[0], halves[1])
        o_ref[:, csl] = (o * z_ref[:, csl].astype(_F32)).astype(_BF16)


def _nbr_attention(proj3, bm_tab, kr):
    b, seq, _ = proj3.shape
    rows = seq // GRID_W
    n_grp = rows // NA_ROWS
    tok = NA_ROWS * GRID_W
    sig = lambda g: tuple(
        (0 <= _na_window_start(g, rows, kr) + ko - _na_first_key_row(g * NA_ROWS + rr, rows, kr) < kr,
         _na_window_start(g, rows, kr) + ko - (g * NA_ROWS + rr))
        for rr in range(NA_ROWS) for ko in range(NA_KROWS))
    variant_of = lambda g: (g > 0) + (g == n_grp - 1)
    reps = _na_variant_groups(rows)
    assert all(sig(g) == sig(reps[variant_of(g)]) for g in range(n_grp))
    grp = lambda gi: (lambda bi, g: (bi, g, gi))
    full = lambda gi: (lambda bi, g: (bi, 0, gi))
    return pl.pallas_call(
        lambda *refs: _nbr_kernel(*refs, rows=rows, kr=kr),
        grid=(b, n_grp),
        in_specs=[
            pl.BlockSpec((None, tok, W_B), grp(4)),
            pl.BlockSpec((None, seq, W_B), full(5)),
            pl.BlockSpec((None, seq, W_B), full(6)),
            pl.BlockSpec((None, tok, W_B), grp(7)),
            pl.BlockSpec((None,) + bm_tab.shape[1:],
                         lambda bi, g: ((g > 0).astype(jnp.int32) + (g == n_grp - 1).astype(jnp.int32),
                                        0, 0, 0)),
        ],
        out_specs=pl.BlockSpec((None, tok, W_B), lambda bi, g: (bi, g, 0)),
        out_shape=jax.ShapeDtypeStruct((b, seq, W_B), _BF16),
        scratch_shapes=[pltpu.VMEM((seq, W_B), _BF16), pltpu.VMEM((seq, W_B), _BF16)],
        compiler_params=pltpu.CompilerParams(
            dimension_semantics=("arbitrary", "arbitrary"), vmem_limit_bytes=VMEM_LIMIT),
        name="nbr_attn",
    )(proj3, proj3, proj3, proj3, bm_tab)


def _out_kernel(x_ref, ya_ref, yb_ref, p_ref, wo_ref, wg_ref, wp_ref, o_ref):
    y = (jnp.dot(ya_ref[...], wo_ref[:W_A, :], preferred_element_type=_F32)
         + jnp.dot(yb_ref[...], wo_ref[W_A:, :], preferred_element_type=_F32))
    x1 = x_ref[...] + y
    gate = jax.nn.sigmoid(jnp.dot(x1.astype(_BF16), wg_ref[...], preferred_element_type=_F32))
    ple = jnp.dot(p_ref[...].astype(_BF16), wp_ref[...], preferred_element_type=_F32)
    o_ref[...] = x1 + gate * ple


def _out(x2, ya2, yb2, p2, wo, wg, wp):
    n = x2.shape[0]
    tile = lambda w: pl.BlockSpec((TM_OUT, w), lambda i: (i, 0))
    whole = lambda a: pl.BlockSpec(a.shape, lambda i: (0, 0))
    return pl.pallas_call(
        _out_kernel,
        grid=(n // TM_OUT,),
        in_specs=[tile(D_MODEL), tile(W_A), tile(W_B), tile(PLE_DIM),
                  whole(wo), whole(wg), whole(wp)],
        out_specs=tile(D_MODEL),
        out_shape=jax.ShapeDtypeStruct((n, D_MODEL), _F32),
        compiler_params=pltpu.CompilerParams(
            dimension_semantics=("arbitrary",), vmem_limit_bytes=VMEM_LIMIT),
        name="out_proj",
    )(x2, ya2, yb2, p2, wo, wg, wp)


def _segment_ones():
    idx = np.arange(GROUP_W // 2) // HEAD_DIM
    return jnp.asarray((idx[:, None] == idx[None, :]).astype(np.float32), dtype=_BF16)


def kernel(x, p, norm_g, w_in, w_out, q_norm_a, k_norm_a, lam_q1, lam_k1, lam_q2, lam_k2,
           subln_g, t5_bias, q_norm_b, k_norm_b, na_rpb, w_ple_gate, w_ple_proj):
    b, seq, _ = x.shape
    depth = w_in.shape[0]
    n = b * seq
    rows = seq // GRID_W
    kr = min(WIN_R, rows)
    assert kr == WIN_R and rows % NA_ROWS == 0 and seq % TQ == 0 and n % TM_PROJ == 0

    seg = _segment_ones()
    bias_tab = _t5_table(t5_bias.astype(_F32), seq)
    ones_w = jnp.ones((1, GROUP_W), _F32)
    for i in range(depth):
        lam_init = 0.8 - 0.6 * math.exp(-0.3 * i)
        tile_a = lambda g, s: jnp.tile(g.astype(_F32), GROUP_W // DH_DIFF)[None, :] * s
        gains = jnp.concatenate([
            tile_a(q_norm_a[i], DH_DIFF ** -0.5 * LOG2E), tile_a(k_norm_a[i], 1.0), ones_w, ones_w,
            tile_a(q_norm_b[i], DH_NA ** -0.5 * LOG2E), tile_a(k_norm_b[i], 1.0), ones_w, ones_w],
            axis=-1)
        proj = _proj(x.reshape(n, D_MODEL), norm_g[i][None, :].astype(_F32),
                     w_in[i].astype(_BF16), seg, gains)
        proj3 = proj.reshape(b, seq, D_IN)
        lam_vecs = jnp.stack([lam_q1[i], lam_k1[i], lam_q2[i], lam_k2[i]]).astype(_F32)
        ya = _diff_attention(proj3, lam_vecs, bias_tab, subln_g[i][None, :].astype(_F32), lam_init)
        bm_tab = _na_table(na_rpb[i].astype(_F32), rows, kr)
        yb = _nbr_attention(proj3, bm_tab, kr)
        x = _out(x.reshape(n, D_MODEL), ya.reshape(n, W_A), yb.reshape(n, W_B),
                 p[i].reshape(n, PLE_DIM), w_out[i].astype(_BF16),
                 w_ple_gate[i].astype(_BF16), w_ple_proj[i].astype(_BF16)).reshape(b, seq, D_MODEL)
    return x
```

```python
import math

import jax
import jax.numpy as jnp
import numpy as np
from jax import lax
from jax.experimental import pallas as pl
from jax.experimental.pallas import tpu as pltpu

D_MODEL = 1024
PLE_DIM = 256
GRID_W = 64
H_DIFF = 4
DH_DIFF = 64
DV_DIFF = 2 * DH_DIFF
H_NA = 8
DH_NA = 64
WIN_R = 8
WIN_C = 16
N_BUCKETS = 32
MAX_DIST = 128
W_A = H_DIFF * DV_DIFF
W_B = H_NA * DH_NA
D_IN = 4 * W_A + 4 * W_B
EPS = 1e-6
NEG = -1e30

GROUP_W = 512
NORM_GROUPS = (0, 1, 4, 5)
GATE_GROUPS = (3, 7)
HEAD_DIM = 64

TM_PROJ = 512
TM_OUT = 512
TQ = 256
KEY_BLK = 128
ROW_CHUNK = 32
NA_ROWS = 4
NA_KROWS = NA_ROWS + WIN_R
LOG2E = math.log2(math.e)
VMEM_LIMIT = 56 * 1024 * 1024

_F32 = jnp.float32
_BF16 = jnp.bfloat16


def _t5_thresholds():
    half = N_BUCKETS // 2
    max_exact = half // 2
    steps = half - max_exact
    ratio = MAX_DIST // max_exact
    thr = []
    for j in range(1, steps):
        n = max_exact
        while n ** steps < (max_exact ** steps) * (ratio ** j):
            n += 1
        thr.append(n)
    return tuple(thr)


def _t5_table_rows(seq):
    return 2 * seq - KEY_BLK


def _t5_table_kernel(tb_ref, o_ref, *, seq):
    h = pl.program_id(0)
    half = N_BUCKETS // 2
    max_exact = half // 2
    blk = (KEY_BLK, KEY_BLK)
    far_pos = tb_ref[(N_BUCKETS - 1) * H_DIFF + h] * LOG2E
    far_neg = tb_ref[(half - 1) * H_DIFF + h] * LOG2E
    for rb in range(_t5_table_rows(seq) // KEY_BLK):
        d_lo = KEY_BLK * rb - (seq - KEY_BLK)
        if -(d_lo + KEY_BLK - 1) >= MAX_DIST:
            val = jnp.full(blk, far_pos, _F32)
        elif (KEY_BLK - 1) - d_lo <= -MAX_DIST:
            val = jnp.full(blk, far_neg, _F32)
        else:
            d = lax.broadcasted_iota(jnp.int32, blk, 0) + d_lo
            rel = lax.broadcasted_iota(jnp.int32, blk, 1) - d
            n = jnp.abs(rel)
            large = jnp.full_like(n, max_exact)
            for thr in _t5_thresholds():
                large = large + (n >= thr).astype(jnp.int32)
            bucket = jnp.where(rel > 0, half, 0) + jnp.where(n < max_exact, n, large)
            val = jnp.zeros(blk, _F32)
            for j in range(N_BUCKETS):
                val = jnp.where(bucket == j, tb_ref[j * H_DIFF + h] * LOG2E, val)
        o_ref[rb * KEY_BLK:(rb + 1) * KEY_BLK, :] = val.astype(_BF16)


def _t5_table(t5_bias, seq):
    rows = _t5_table_rows(seq)
    return pl.pallas_call(
        lambda tb, o: _t5_table_kernel(tb, o, seq=seq),
        grid=(H_DIFF,),
        in_specs=[pl.BlockSpec(memory_space=pltpu.SMEM)],
        out_specs=pl.BlockSpec((None, rows, KEY_BLK), lambda h: (h, 0, 0)),
        out_shape=jax.ShapeDtypeStruct((H_DIFF, rows, KEY_BLK), _BF16),
        name="t5_table",
    )(t5_bias.reshape(-1))


def _na_first_key_row(r, rows, kr):
    return min(max(r - kr // 2, 0), rows - kr)


def _na_window_start(g, rows, kr):
    return min(max(g * NA_ROWS - kr // 2, 0), rows - NA_KROWS)


def _na_variant_groups(rows):
    return (0, 1, rows // NA_ROWS - 1)


def _na_table_kernel(rpb_ref, o_ref, *, rows, kr):
    h = pl.program_id(0)
    n_dr = 2 * WIN_R - 1
    n_dc = 2 * WIN_C - 1
    pair = (GRID_W, 2 * GRID_W)
    cq = lax.broadcasted_iota(jnp.int32, pair, 0)
    lane = lax.broadcasted_iota(jnp.int32, pair, 1)
    ck = lane & (GRID_W - 1)
    dc = jnp.clip(ck - cq, -(WIN_C - 1), WIN_C - 1) + (WIN_C - 1)
    cs = jnp.clip(cq - WIN_C // 2, 0, GRID_W - WIN_C)
    col_ok = (ck >= cs) & (ck < cs + WIN_C)
    left = lane < GRID_W
    neg = jnp.full(pair, NEG, _F32)
    tiles = {}

    def tile(dr):
        if dr not in tiles:
            acc = jnp.zeros(pair, _F32)
            for j in range(n_dc):
                acc = jnp.where(dc == j, rpb_ref[(h * n_dr + dr) * n_dc + j] * LOG2E, acc)
            tiles[dr] = jnp.where(col_ok, acc, NEG)
        return tiles[dr]

    for v, g in enumerate(_na_variant_groups(rows)):
        ws = _na_window_start(g, rows, kr)
        for rr in range(NA_ROWS):
            r = g * NA_ROWS + rr
            rs = _na_first_key_row(r, rows, kr)
            for kp in range(NA_KROWS // 2):
                parts = []
                for ko in (2 * kp, 2 * kp + 1):
                    in_win = 0 <= ws + ko - rs < kr
                    parts.append(tile(ws + ko - r + WIN_R - 1) if in_win else neg)
                val = parts[0] if parts[0] is parts[1] else jnp.where(left, parts[0], parts[1])
                o_ref[v, rr * GRID_W:(rr + 1) * GRID_W, kp * 2 * GRID_W:(kp + 1) * 2 * GRID_W] = val


def _na_table(rpb, rows, kr):
    n_var = len(_na_variant_groups(rows))
    shape = (n_var, H_NA, NA_ROWS * GRID_W, NA_KROWS * GRID_W)
    return pl.pallas_call(
        lambda r, o: _na_table_kernel(r, o, rows=rows, kr=kr),
        grid=(H_NA,),
        in_specs=[pl.BlockSpec(memory_space=pltpu.SMEM)],
        out_specs=pl.BlockSpec((n_var, None) + shape[2:], lambda h: (0, h, 0, 0)),
        out_shape=jax.ShapeDtypeStruct(shape, _F32),
        name="na_table",
    )(rpb.reshape(-1))


def _proj_kernel(x_ref, ng_ref, w_ref, seg_ref, gain_ref, o_ref):
    x = x_ref[...]
    ms = jnp.mean(x * x, axis=-1, keepdims=True)
    xn = (x * lax.rsqrt(ms + EPS) * ng_ref[...]).astype(_BF16)
    seg = seg_ref[...]
    half_w = GROUP_W // 2
    for j in range(D_IN // GROUP_W):
        sl = slice(j * GROUP_W, (j + 1) * GROUP_W)
        acc = jnp.dot(xn, w_ref[:, sl], preferred_element_type=_F32)
        if j in NORM_GROUPS:
            sq = (acc * acc).astype(_BF16)
            ss = jnp.concatenate(
                [jnp.dot(sq[:, :half_w], seg, preferred_element_type=_F32),
                 jnp.dot(sq[:, half_w:], seg, preferred_element_type=_F32)], axis=-1)
            acc = acc * lax.rsqrt(ss * (1.0 / HEAD_DIM) + EPS) * gain_ref[:, sl]
        elif j in GATE_GROUPS:
            acc = acc * jax.nn.sigmoid(acc)
        o_ref[:, sl] = acc.astype(_BF16)


def _proj(x2, norm_g, w_bf, seg, gains):
    n = x2.shape[0]
    return pl.pallas_call(
        _proj_kernel,
        grid=(n // TM_PROJ,),
        in_specs=[
            pl.BlockSpec((TM_PROJ, D_MODEL), lambda i: (i, 0)),
            pl.BlockSpec((1, D_MODEL), lambda i: (0, 0)),
            pl.BlockSpec((D_MODEL, D_IN), lambda i: (0, 0)),
            pl.BlockSpec(seg.shape, lambda i: (0, 0)),
            pl.BlockSpec((1, D_IN), lambda i: (0, 0)),
        ],
        out_specs=pl.BlockSpec((TM_PROJ, D_IN), lambda i: (i, 0)),
        out_shape=jax.ShapeDtypeStruct((n, D_IN), _BF16),
        compiler_params=pltpu.CompilerParams(
            dimension_semantics=("arbitrary",), vmem_limit_bytes=VMEM_LIMIT),
        name="proj",
    )(x2, norm_g, w_bf, seg, gains)


def _diff_kernel(lam_ref, q_ref, k_ref, v_ref, z_ref, tab_ref, g_ref, o_ref,
                 kaug_scr, s_even, s_odd, l_even, l_odd, e1_even, e2_even, e1_odd, e2_odd,
                 *, seq, n_steps, lam_init):
    nq = seq // TQ
    nkb = seq // KEY_BLK
    t = pl.program_id(0)
    t_a = jnp.minimum(t, n_steps - 1)
    i_a = t_a % nq
    h_a = (t_a // nq) % H_DIFF
    nt = (((1,), (1,)), ((), ()))

    @pl.when(t == 0)
    def _():
        row = lax.broadcasted_iota(jnp.int32, (KEY_BLK, KEY_BLK), 0)
        col = lax.broadcasted_iota(jnp.int32, (KEY_BLK, KEY_BLK), 1)
        eye = (row == col).astype(_BF16)
        for j in range(2 * nkb):
            kaug_scr[j * KEY_BLK:(j + 1) * KEY_BLK, KEY_BLK:] = eye
        s_odd[...] = jnp.zeros_like(s_odd)
        e1_even[...] = jnp.zeros_like(e1_even)
        e2_even[...] = jnp.zeros_like(e2_even)
        l_even[...] = jnp.ones_like(l_even)

    @pl.when(i_a == 0)
    def _():
        lane = lax.broadcasted_iota(jnp.int32, (KEY_BLK, DV_DIFF), 1)
        zero = jnp.zeros((KEY_BLK, DV_DIFF), _BF16)
        for j in range(nkb):
            kj = k_ref[j * KEY_BLK:(j + 1) * KEY_BLK, :]
            kaug_scr[2 * j * KEY_BLK:(2 * j + 1) * KEY_BLK, :KEY_BLK] = jnp.where(lane < DH_DIFF, kj, zero)
            kaug_scr[(2 * j + 1) * KEY_BLK:(2 * j + 2) * KEY_BLK, :KEY_BLK] = jnp.where(lane >= DH_DIFF, kj, zero)

    def stages(s_w, s_r, e1_w, e2_w, l_w, e1_r, e2_r, l_r):
        q = q_ref[...]

        def score_block(j):
            off = pl.multiple_of(i_a * TQ + (seq - KEY_BLK - KEY_BLK * j), KEY_BLK)
            lhs = jnp.concatenate([q, tab_ref[h_a, pl.ds(off, TQ), :]], axis=1)
            w = kaug_scr[2 * j * KEY_BLK:(2 * j + 2) * KEY_BLK, :]
            sj = lax.dot_general(lhs, w, nt, preferred_element_type=_F32)
            s_w[:, 2 * j * KEY_BLK:(2 * j + 2) * KEY_BLK] = sj
            edge = jnp.minimum(sj[:ROW_CHUNK, :KEY_BLK], sj[TQ - ROW_CHUNK:, KEY_BLK:])
            return jnp.minimum(edge, -jnp.inf)

        def softmax_chunk(rc, anchor):
            rows = slice(rc * ROW_CHUNK, (rc + 1) * ROW_CHUNK)
            tile = lambda mp, j: s_r[rows, (2 * j + mp) * KEY_BLK:(2 * j + mp + 1) * KEY_BLK]
            mx1, mx2 = tile(0, 0), tile(1, 0)
            if anchor is not None:
                mx1 = jnp.maximum(mx1, anchor)
            for j in range(1, nkb):
                mx1 = jnp.maximum(mx1, tile(0, j))
                mx2 = jnp.maximum(mx2, tile(1, j))
            m1 = jnp.broadcast_to(jnp.max(mx1, axis=-1, keepdims=True), (ROW_CHUNK, KEY_BLK))
            m2 = jnp.broadcast_to(jnp.max(mx2, axis=-1, keepdims=True), (ROW_CHUNK, KEY_BLK))
            l1 = jnp.zeros((ROW_CHUNK, KEY_BLK), _F32)
            l2 = jnp.zeros((ROW_CHUNK, KEY_BLK), _F32)
            for j in range(nkb):
                e1 = jnp.exp2(tile(0, j) - m1)
                e2 = jnp.exp2(tile(1, j) - m2)
                l1 = l1 + e1
                l2 = l2 + e2
                e1_w[rows, j * KEY_BLK:(j + 1) * KEY_BLK] = e1.astype(_BF16)
                e2_w[rows, j * KEY_BLK:(j + 1) * KEY_BLK] = e2.astype(_BF16)
            l_w[0, rows, :] = jnp.sum(l1, axis=-1, keepdims=True)
            l_w[1, rows, :] = jnp.sum(l2, axis=-1, keepdims=True)

        n_chunks = TQ // ROW_CHUNK
        per_chunk = nkb // n_chunks
        anchor = None
        for rc in range(n_chunks):
            softmax_chunk(rc, anchor)
            anchor = None
            for j in range(rc * per_chunk, (rc + 1) * per_chunk):
                a_j = score_block(j)
                anchor = a_j if anchor is None else jnp.minimum(anchor, a_j)

        lv = lam_ref[...]
        lam = (jnp.exp(jnp.sum(lv[0:1] * lv[1:2], axis=-1, keepdims=True))
               - jnp.exp(jnp.sum(lv[2:3] * lv[3:4], axis=-1, keepdims=True)) + lam_init)
        l1 = l_r[0]
        l2 = l_r[1]
        r = jnp.broadcast_to(lam * l1 / l2, (TQ, 2 * KEY_BLK)).astype(_BF16)
        acc = jnp.zeros((TQ, DV_DIFF), _F32)
        for jj in range(seq // (2 * KEY_BLK)):
            sl = slice(jj * 2 * KEY_BLK, (jj + 1) * 2 * KEY_BLK)
            a = e1_r[:, sl] - r * e2_r[:, sl]
            acc = acc + jnp.dot(a, v_ref[sl, :], preferred_element_type=_F32)
        o = acc * (1.0 / l1)
        ms = jnp.mean(o * o, axis=-1, keepdims=True)
        o = o * lax.rsqrt(ms + EPS) * g_ref[...] * (1.0 - lam_init)
        o_ref[...] = (o * z_ref[...].astype(_F32)).astype(_BF16)

    @pl.when(t % 2 == 0)
    def _():
        stages(s_even, s_odd, e1_odd, e2_odd, l_odd, e1_even, e2_even, l_even)

    @pl.when(t % 2 == 1)
    def _():
        stages(s_odd, s_even, e1_even, e2_even, l_even, e1_odd, e2_odd, l_odd)


def _diff_attention(proj3, lam_vecs, tab, subln_g, lam_init):
    b, seq, _ = proj3.shape
    nq = seq // TQ
    n_steps = b * H_DIFF * nq

    def split(t):
        return t // (H_DIFF * nq), (t // nq) % H_DIFF, t % nq

    def cur(col0, whole):
        def index(t):
            bi, h, i = split(jnp.minimum(t, n_steps - 1))
            return bi, (0 if whole else i), col0 + h
        return index

    def prev(col0, whole):
        def index(t):
            bi, h, i = split(jnp.clip(t - 2, 0, n_steps - 1))
            return bi, (0 if whole else i), col0 + h
        return index

    s_shape = pltpu.VMEM((TQ, 2 * seq + KEY_BLK), _F32)
    e_shape = pltpu.VMEM((TQ, seq + KEY_BLK), _BF16)
    l_shape = pltpu.VMEM((2, TQ, 1), _F32)
    return pl.pallas_call(
        lambda *refs: _diff_kernel(*refs, seq=seq, n_steps=n_steps, lam_init=lam_init),
        grid=(n_steps + 2,),
        in_specs=[
            pl.BlockSpec(lam_vecs.shape, lambda t: (0, 0)),
            pl.BlockSpec((None, TQ, DV_DIFF), cur(0, False)),
            pl.BlockSpec((None, seq, DV_DIFF), cur(H_DIFF, True)),
            pl.BlockSpec((None, seq, DV_DIFF), prev(2 * H_DIFF, True)),
            pl.BlockSpec((None, TQ, DV_DIFF), prev(3 * H_DIFF, False)),
            pl.BlockSpec(tab.shape, lambda t: (0, 0, 0)),
            pl.BlockSpec((1, DV_DIFF), lambda t: (0, 0)),
        ],
        out_specs=pl.BlockSpec((None, TQ, DV_DIFF), prev(0, False)),
        out_shape=jax.ShapeDtypeStruct((b, seq, W_A), _BF16),
        scratch_shapes=[
            pltpu.VMEM((2 * seq, 2 * KEY_BLK), _BF16),
            s_shape, s_shape,
            l_shape, l_shape, e_shape, e_shape, e_shape, e_shape,
        ],
        compiler_params=pltpu.CompilerParams(
            dimension_semantics=("arbitrary",), vmem_limit_bytes=VMEM_LIMIT),
        name="diff_attn",
    )(lam_vecs, proj3, proj3, proj3, proj3, tab, subln_g)


def _nbr_kernel(q_ref, k_ref, v_ref, z_ref, bm_ref, o_ref, ke_scr, ko_scr, *, rows, kr):
    g = pl.program_id(1)
    pair_w = 2 * DH_NA

    @pl.when(g == 0)
    def _():
        k = k_ref[...]
        lane = lax.broadcasted_iota(jnp.int32, k.shape, 1) & (pair_w - 1)
        zero = jnp.zeros_like(k)
        ke_scr[...] = jnp.where(lane < DH_NA, k, zero)
        ko_scr[...] = jnp.where(lane >= DH_NA, k, zero)

    nt = (((1,), (1,)), ((), ()))
    win = NA_KROWS * GRID_W
    tok = NA_ROWS * GRID_W
    ws = jnp.clip(g * NA_ROWS - kr // 2, 0, rows - NA_KROWS)
    start = pl.multiple_of(ws * GRID_W, GRID_W)
    out_lane = lax.broadcasted_iota(jnp.int32, (tok, pair_w), 1)
    probs = []
    for h in range(H_NA):
        csl = slice((h // 2) * pair_w, (h // 2 + 1) * pair_w)
        kw = (ke_scr, ko_scr)[h % 2][pl.ds(start, win), csl]
        s = lax.dot_general(q_ref[:, csl], kw, nt, preferred_element_type=_F32)
        s = s + bm_ref[h]
        e = jnp.exp2(s - jnp.max(s, axis=-1, keepdims=True))
        probs.append((e.astype(_BF16), 1.0 / jnp.sum(e, axis=-1, keepdims=True)))
    for hp in range(H_NA // 2):
        csl = slice(hp * pair_w, (hp + 1) * pair_w)
        vw = v_ref[pl.ds(start, win), csl]
        halves = [jnp.dot(e, vw, preferred_element_type=_F32) * inv
                  for e, inv in probs[2 * hp:2 * hp + 2]]
        o = jnp.where(out_lane < DH_NA, halves[0], halves[1])
        o_ref[:, csl] = (o * z_ref[:, csl].astype(_F32)).astype(_BF16)


def _nbr_attention(proj3, bm_tab, kr):
    b, seq, _ = proj3.shape
    rows = seq // GRID_W
    n_grp = rows // NA_ROWS
    tok = NA_ROWS * GRID_W
    sig = lambda g: tuple(
        (0 <= _na_window_start(g, rows, kr) + ko - _na_first_key_row(g * NA_ROWS + rr, rows, kr) < kr,
         _na_window_start(g, rows, kr) + ko - (g * NA_ROWS + rr))
        for rr in range(NA_ROWS) for ko in range(NA_KROWS))
    variant_of = lambda g: (g > 0) + (g == n_grp - 1)
    reps = _na_variant_groups(rows)
    assert all(sig(g) == sig(reps[variant_of(g)]) for g in range(n_grp))
    grp = lambda gi: (lambda bi, g: (bi, g, gi))
    full = lambda gi: (lambda bi, g: (bi, 0, gi))
    return pl.pallas_call(
        lambda *refs: _nbr_kernel(*refs, rows=rows, kr=kr),
        grid=(b, n_grp),
        in_specs=[
            pl.BlockSpec((None, tok, W_B), grp(4)),
            pl.BlockSpec((None, seq, W_B), full(5)),
            pl.BlockSpec((None, seq, W_B), full(6)),
            pl.BlockSpec((None, tok, W_B), grp(7)),
            pl.BlockSpec((None,) + bm_tab.shape[1:],
                         lambda bi, g: ((g > 0).astype(jnp.int32) + (g == n_grp - 1).astype(jnp.int32),
                                        0, 0, 0)),
        ],
        out_specs=pl.BlockSpec((None, tok, W_B), lambda bi, g: (bi, g, 0)),
        out_shape=jax.ShapeDtypeStruct((b, seq, W_B), _BF16),
        scratch_shapes=[pltpu.VMEM((seq, W_B), _BF16), pltpu.VMEM((seq, W_B), _BF16)],
        compiler_params=pltpu.CompilerParams(
            dimension_semantics=("arbitrary", "arbitrary"), vmem_limit_bytes=VMEM_LIMIT),
        name="nbr_attn",
    )(proj3, proj3, proj3, proj3, bm_tab)


def _out_kernel(x_ref, ya_ref, yb_ref, p_ref, wo_ref, wg_ref, wp_ref, o_ref):
    y = (jnp.dot(ya_ref[...], wo_ref[:W_A, :], preferred_element_type=_F32)
         + jnp.dot(yb_ref[...], wo_ref[W_A:, :], preferred_element_type=_F32))
    x1 = x_ref[...] + y
    gate = jax.nn.sigmoid(jnp.dot(x1.astype(_BF16), wg_ref[...], preferred_element_type=_F32))
    ple = jnp.dot(p_ref[...].astype(_BF16), wp_ref[...], preferred_element_type=_F32)
    o_ref[...] = x1 + gate * ple


def _out(x2, ya2, yb2, p2, wo, wg, wp):
    n = x2.shape[0]
    tile = lambda w: pl.BlockSpec((TM_OUT, w), lambda i: (i, 0))
    whole = lambda a: pl.BlockSpec(a.shape, lambda i: (0, 0))
    return pl.pallas_call(
        _out_kernel,
        grid=(n // TM_OUT,),
        in_specs=[tile(D_MODEL), tile(W_A), tile(W_B), tile(PLE_DIM),
                  whole(wo), whole(wg), whole(wp)],
        out_specs=tile(D_MODEL),
        out_shape=jax.ShapeDtypeStruct((n, D_MODEL), _F32),
        compiler_params=pltpu.CompilerParams(
            dimension_semantics=("arbitrary",), vmem_limit_bytes=VMEM_LIMIT),
        name="out_proj",
    )(x2, ya2, yb2, p2, wo, wg, wp)


def _segment_ones():
    idx = np.arange(GROUP_W // 2) // HEAD_DIM
    return jnp.asarray((idx[:, None] == idx[None, :]).astype(np.float32), dtype=_BF16)


def kernel(x, p, norm_g, w_in, w_out, q_norm_a, k_norm_a, lam_q1, lam_k1, lam_q2, lam_k2,
           subln_g, t5_bias, q_norm_b, k_norm_b, na_rpb, w_ple_gate, w_ple_proj):
    b, seq, _ = x.shape
    depth = w_in.shape[0]
    n = b * seq
    rows = seq // GRID_W
    kr = min(WIN_R, rows)
    assert kr == WIN_R and rows % NA_ROWS == 0 and seq % TQ == 0 and n % TM_PROJ == 0

    seg = _segment_ones()
    bias_tab = _t5_table(t5_bias.astype(_F32), seq)
    ones_w = jnp.ones((1, GROUP_W), _F32)
    for i in range(depth):
        lam_init = 0.8 - 0.6 * math.exp(-0.3 * i)
        tile_a = lambda g, s: jnp.tile(g.astype(_F32), GROUP_W // DH_DIFF)[None, :] * s
        gains = jnp.concatenate([
            tile_a(q_norm_a[i], DH_DIFF ** -0.5 * LOG2E), tile_a(k_norm_a[i], 1.0), ones_w, ones_w,
            tile_a(q_norm_b[i], DH_NA ** -0.5 * LOG2E), tile_a(k_norm_b[i], 1.0), ones_w, ones_w],
            axis=-1)
        proj = _proj(x.reshape(n, D_MODEL), norm_g[i][None, :].astype(_F32),
                     w_in[i].astype(_BF16), seg, gains)
        proj3 = proj.reshape(b, seq, D_IN)
        lam_vecs = jnp.stack([lam_q1[i], lam_k1[i], lam_q2[i], lam_k2[i]]).astype(_F32)
        ya = _diff_attention(proj3, lam_vecs, bias_tab, subln_g[i][None, :].astype(_F32), lam_init)
        bm_tab = _na_table(na_rpb[i].astype(_F32), rows, kr)
        yb = _nbr_attention(proj3, bm_tab, kr)
        x = _out(x.reshape(n, D_MODEL), ya.reshape(n, W_A), yb.reshape(n, W_B),
                 p[i].reshape(n, PLE_DIM), w_out[i].astype(_BF16),
                 w_ple_gate[i].astype(_BF16), w_ple_proj[i].astype(_BF16)).reshape(b, seq, D_MODEL)
    return x
```

```python
import math

import jax
import jax.numpy as jnp
import numpy as np
from jax import lax
from jax.experimental import pallas as pl
from jax.experimental.pallas import tpu as pltpu

D_MODEL = 1024
PLE_DIM = 256
GRID_W = 64
H_DIFF = 4
DH_DIFF = 64
DV_DIFF = 2 * DH_DIFF
H_NA = 8
DH_NA = 64
WIN_R = 8
WIN_C = 16
N_BUCKETS = 32
MAX_DIST = 128
W_A = H_DIFF * DV_DIFF
W_B = H_NA * DH_NA
D_IN = 4 * W_A + 4 * W_B
EPS = 1e-6
NEG = -1e30

GROUP_W = 512
NORM_GROUPS = (0, 1, 4, 5)
GATE_GROUPS = (3, 7)
HEAD_DIM = 64

TM_PROJ = 512
TM_OUT = 512
TQ = 512
KEY_BLK = 128
SUBLANES = 8
ANCHOR_LAG = 99
NA_ROWS = 4
NA_KROWS = NA_ROWS + WIN_R
LOG2E = math.log2(math.e)
VMEM_LIMIT = 56 * 1024 * 1024

_F32 = jnp.float32
_BF16 = jnp.bfloat16


def _t5_thresholds():
    half = N_BUCKETS // 2
    max_exact = half // 2
    steps = half - max_exact
    ratio = MAX_DIST // max_exact
    thr = []
    for j in range(1, steps):
        n = max_exact
        while n ** steps < (max_exact ** steps) * (ratio ** j):
            n += 1
        thr.append(n)
    return tuple(thr)


def _t5_table_rows(seq):
    return 2 * seq - KEY_BLK


def _t5_table_kernel(tb_ref, o_ref, *, seq):
    h = pl.program_id(0)
    half = N_BUCKETS // 2
    max_exact = half // 2
    blk = (KEY_BLK, KEY_BLK)
    far_pos = tb_ref[(N_BUCKETS - 1) * H_DIFF + h] * LOG2E
    far_neg = tb_ref[(half - 1) * H_DIFF + h] * LOG2E
    for rb in range(_t5_table_rows(seq) // KEY_BLK):
        d_lo = KEY_BLK * rb - (seq - KEY_BLK)
        if -(d_lo + KEY_BLK - 1) >= MAX_DIST:
            val = jnp.full(blk, far_pos, _F32)
        elif (KEY_BLK - 1) - d_lo <= -MAX_DIST:
            val = jnp.full(blk, far_neg, _F32)
        else:
            d = lax.broadcasted_iota(jnp.int32, blk, 0) + d_lo
            rel = lax.broadcasted_iota(jnp.int32, blk, 1) - d
            n = jnp.abs(rel)
            large = jnp.full_like(n, max_exact)
            for thr in _t5_thresholds():
                large = large + (n >= thr).astype(jnp.int32)
            bucket = jnp.where(rel > 0, half, 0) + jnp.where(n < max_exact, n, large)
            val = jnp.zeros(blk, _F32)
            for j in range(N_BUCKETS):
                val = jnp.where(bucket == j, tb_ref[j * H_DIFF + h] * LOG2E, val)
        o_ref[rb * KEY_BLK:(rb + 1) * KEY_BLK, :] = val.astype(_BF16)


def _t5_table(t5_bias, seq):
    rows = _t5_table_rows(seq)
    return pl.pallas_call(
        lambda tb, o: _t5_table_kernel(tb, o, seq=seq),
        grid=(H_DIFF,),
        in_specs=[pl.BlockSpec(memory_space=pltpu.SMEM)],
        out_specs=pl.BlockSpec((None, rows, KEY_BLK), lambda h: (h, 0, 0)),
        out_shape=jax.ShapeDtypeStruct((H_DIFF, rows, KEY_BLK), _BF16),
        name="t5_table",
    )(t5_bias.reshape(-1))


def _na_first_key_row(r, rows, kr):
    return min(max(r - kr // 2, 0), rows - kr)


def _na_window_start(g, rows, kr):
    return min(max(g * NA_ROWS - kr // 2, 0), rows - NA_KROWS)


def _na_variant_groups(rows):
    return (0, 1, rows // NA_ROWS - 1)


def _na_table_kernel(rpb_ref, o_ref, *, rows, kr):
    h = pl.program_id(0)
    n_dr = 2 * WIN_R - 1
    n_dc = 2 * WIN_C - 1
    pair = (GRID_W, 2 * GRID_W)
    cq = lax.broadcasted_iota(jnp.int32, pair, 0)
    lane = lax.broadcasted_iota(jnp.int32, pair, 1)
    ck = lane & (GRID_W - 1)
    dc = jnp.clip(ck - cq, -(WIN_C - 1), WIN_C - 1) + (WIN_C - 1)
    cs = jnp.clip(cq - WIN_C // 2, 0, GRID_W - WIN_C)
    col_ok = (ck >= cs) & (ck < cs + WIN_C)
    left = lane < GRID_W
    neg = jnp.full(pair, NEG, _F32)
    tiles = {}

    def tile(dr):
        if dr not in tiles:
            acc = jnp.zeros(pair, _F32)
            for j in range(n_dc):
                acc = jnp.where(dc == j, rpb_ref[(h * n_dr + dr) * n_dc + j] * LOG2E, acc)
            tiles[dr] = jnp.where(col_ok, acc, NEG)
        return tiles[dr]

    for v, g in enumerate(_na_variant_groups(rows)):
        ws = _na_window_start(g, rows, kr)
        for rr in range(NA_ROWS):
            r = g * NA_ROWS + rr
            rs = _na_first_key_row(r, rows, kr)
            for kp in range(NA_KROWS // 2):
                parts = []
                for ko in (2 * kp, 2 * kp + 1):
                    in_win = 0 <= ws + ko - rs < kr
                    parts.append(tile(ws + ko - r + WIN_R - 1) if in_win else neg)
                val = parts[0] if parts[0] is parts[1] else jnp.where(left, parts[0], parts[1])
                o_ref[v, rr * GRID_W:(rr + 1) * GRID_W, kp * 2 * GRID_W:(kp + 1) * 2 * GRID_W] = val


def _na_table(rpb, rows, kr):
    n_var = len(_na_variant_groups(rows))
    shape = (n_var, H_NA, NA_ROWS * GRID_W, NA_KROWS * GRID_W)
    return pl.pallas_call(
        lambda r, o: _na_table_kernel(r, o, rows=rows, kr=kr),
        grid=(H_NA,),
        in_specs=[pl.BlockSpec(memory_space=pltpu.SMEM)],
        out_specs=pl.BlockSpec((n_var, None) + shape[2:], lambda h: (0, h, 0, 0)),
        out_shape=jax.ShapeDtypeStruct(shape, _F32),
        name="na_table",
    )(rpb.reshape(-1))


def _proj_kernel(x_ref, ng_ref, w_ref, seg_ref, gain_ref, o_ref):
    x = x_ref[...]
    ms = jnp.mean(x * x, axis=-1, keepdims=True)
    xn = (x * lax.rsqrt(ms + EPS) * ng_ref[...]).astype(_BF16)
    seg = seg_ref[...]
    half_w = GROUP_W // 2
    for j in range(D_IN // GROUP_W):
        sl = slice(j * GROUP_W, (j + 1) * GROUP_W)
        acc = jnp.dot(xn, w_ref[:, sl], preferred_element_type=_F32)
        if j in NORM_GROUPS:
            sq = (acc * acc).astype(_BF16)
            ss = jnp.concatenate(
                [jnp.dot(sq[:, :half_w], seg, preferred_element_type=_F32),
                 jnp.dot(sq[:, half_w:], seg, preferred_element_type=_F32)], axis=-1)
            acc = acc * lax.rsqrt(ss * (1.0 / HEAD_DIM) + EPS) * gain_ref[:, sl]
        elif j in GATE_GROUPS:
            acc = acc * jax.nn.sigmoid(acc)
        o_ref[:, sl] = acc.astype(_BF16)


def _proj(x2, norm_g, w_bf, seg, gains):
    n = x2.shape[0]
    return pl.pallas_call(
        _proj_kernel,
        grid=(n // TM_PROJ,),
        in_specs=[
            pl.BlockSpec((TM_PROJ, D_MODEL), lambda i: (i, 0)),
            pl.BlockSpec((1, D_MODEL), lambda i: (0, 0)),
            pl.BlockSpec((D_MODEL, D_IN), lambda i: (0, 0)),
            pl.BlockSpec(seg.shape, lambda i: (0, 0)),
            pl.BlockSpec((1, D_IN), lambda i: (0, 0)),
        ],
        out_specs=pl.BlockSpec((TM_PROJ, D_IN), lambda i: (i, 0)),
        out_shape=jax.ShapeDtypeStruct((n, D_IN), _BF16),
        compiler_params=pltpu.CompilerParams(
            dimension_semantics=("arbitrary",), vmem_limit_bytes=VMEM_LIMIT),
        name="proj",
    )(x2, norm_g, w_bf, seg, gains)


def _fold_rows(x, op):
    parts = [x[i * SUBLANES:(i + 1) * SUBLANES] for i in range(x.shape[0] // SUBLANES)]
    while len(parts) > 1:
        parts = [op(parts[i], parts[i + 1]) if i + 1 < len(parts) else parts[i]
                 for i in range(0, len(parts), 2)]
    return parts[0]


def _diff_kernel(lam_ref, q_ref, k_ref, v_ref, z_ref, tab_ref, g_ref, o_ref,
                 kaug_scr, vt_scr, s_even, s_odd, mx_even, mx_odd, l_even, l_odd,
                 e1_even, e2_even, e1_odd, e2_odd, *, seq, n_steps, lam_init):
    nq = seq // TQ
    nkb = seq // KEY_BLK
    t = pl.program_id(0)
    t_a = jnp.minimum(t, n_steps - 1)
    i_a = t_a % nq
    h_a = (t_a // nq) % H_DIFF
    i_c = jnp.clip(t - 2, 0, n_steps - 1) % nq
    nt = (((1,), (1,)), ((), ()))

    @pl.when(t == 0)
    def _():
        row = lax.broadcasted_iota(jnp.int32, (KEY_BLK, KEY_BLK), 0)
        col = lax.broadcasted_iota(jnp.int32, (KEY_BLK, KEY_BLK), 1)
        eye = (row == col).astype(_BF16)
        for j in range(2 * nkb):
            kaug_scr[j * KEY_BLK:(j + 1) * KEY_BLK, KEY_BLK:] = eye
        s_odd[...] = jnp.zeros_like(s_odd)
        mx_odd[...] = jnp.zeros_like(mx_odd)
        e1_even[...] = jnp.zeros_like(e1_even)
        e2_even[...] = jnp.zeros_like(e2_even)
        l_even[...] = jnp.ones_like(l_even)

    @pl.when(i_a == 0)
    def _():
        lane = lax.broadcasted_iota(jnp.int32, (KEY_BLK, DV_DIFF), 1)
        zero = jnp.zeros((KEY_BLK, DV_DIFF), _BF16)
        for j in range(nkb):
            kj = k_ref[j * KEY_BLK:(j + 1) * KEY_BLK, :]
            kaug_scr[2 * j * KEY_BLK:(2 * j + 1) * KEY_BLK, :KEY_BLK] = jnp.where(lane < DH_DIFF, kj, zero)
            kaug_scr[(2 * j + 1) * KEY_BLK:(2 * j + 2) * KEY_BLK, :KEY_BLK] = jnp.where(lane >= DH_DIFF, kj, zero)

    @pl.when(i_c == 0)
    def _():
        for j in range(nkb):
            vj = v_ref[j * KEY_BLK:(j + 1) * KEY_BLK, :].astype(_F32)
            vt_scr[:, j * KEY_BLK:(j + 1) * KEY_BLK] = vj.T.astype(_BF16)

    def stages(s_w, mx_w, s_r, mx_r, e1_w, e2_w, l_w, e1_r, e2_r, l_r):
        q = q_ref[...]

        def score_block(j, mx):
            off = pl.multiple_of(i_a * TQ + (seq - KEY_BLK - KEY_BLK * j), KEY_BLK)
            qa = jnp.concatenate([q, tab_ref[h_a, pl.ds(off, TQ), :]], axis=1)
            ka = kaug_scr[2 * j * KEY_BLK:(2 * j + 2) * KEY_BLK, :]
            sj = lax.dot_general(ka, qa, nt, preferred_element_type=_F32)
            s_w[2 * j * KEY_BLK:(2 * j + 2) * KEY_BLK, :] = sj
            p1 = _fold_rows(sj[:KEY_BLK], jnp.maximum)
            p2 = _fold_rows(sj[KEY_BLK:], jnp.maximum)
            mx = (p1, p2) if mx is None else (jnp.maximum(mx[0], p1), jnp.maximum(mx[1], p2))
            edge = jnp.minimum(sj[:SUBLANES], sj[2 * KEY_BLK - SUBLANES:])
            return mx, jnp.minimum(edge, -jnp.inf)

        m1 = jnp.max(mx_r[0], axis=0, keepdims=True)
        m2 = jnp.max(mx_r[1], axis=0, keepdims=True)

        def exp_block(j, ls, anchor):
            if anchor is None:
                m1j, m2j = m1, m2
            else:
                tie = jnp.max(anchor, axis=0, keepdims=True)
                m1j, m2j = jnp.maximum(m1, tie), jnp.maximum(m2, tie)
            e1 = jnp.exp2(s_r[2 * j * KEY_BLK:(2 * j + 1) * KEY_BLK, :] - m1j)
            e2 = jnp.exp2(s_r[(2 * j + 1) * KEY_BLK:(2 * j + 2) * KEY_BLK, :] - m2j)
            e1_w[j * KEY_BLK:(j + 1) * KEY_BLK, :] = e1.astype(_BF16)
            e2_w[j * KEY_BLK:(j + 1) * KEY_BLK, :] = e2.astype(_BF16)
            p1 = _fold_rows(e1, jnp.add)
            p2 = _fold_rows(e2, jnp.add)
            return (p1, p2) if ls is None else (ls[0] + p1, ls[1] + p2)

        def pv_epilogue():
            lv = lam_ref[...]
            lam = (jnp.exp(jnp.sum(lv[0:1] * lv[1:2], axis=-1, keepdims=True))
                   - jnp.exp(jnp.sum(lv[2:3] * lv[3:4], axis=-1, keepdims=True)) + lam_init)
            l1 = jnp.sum(l_r[0], axis=0, keepdims=True)
            l2 = jnp.sum(l_r[1], axis=0, keepdims=True)
            r = (lam * l1 / l2).astype(_BF16)
            a_t = e1_r[...] - r * e2_r[...]
            o_t = jnp.dot(vt_scr[...], a_t, preferred_element_type=_F32) * (1.0 / l1)
            ms = jnp.mean(o_t * o_t, axis=0, keepdims=True)
            o = (o_t * lax.rsqrt(ms + EPS)).T
            o = o * g_ref[...] * (1.0 - lam_init)
            o_ref[...] = (o * z_ref[...].astype(_F32)).astype(_BF16)

        pv_epilogue()
        mx, ls, anchors = None, None, []
        for j in range(nkb):
            ls = exp_block(j, ls, anchors[j - ANCHOR_LAG] if j >= ANCHOR_LAG else None)
            mx, anchor = score_block(j, mx)
            anchors.append(anchor)
        mx_w[0], mx_w[1] = mx
        l_w[0], l_w[1] = ls

    @pl.when(t % 2 == 0)
    def _():
        stages(s_even, mx_even, s_odd, mx_odd, e1_odd, e2_odd, l_odd, e1_even, e2_even, l_even)

    @pl.when(t % 2 == 1)
    def _():
        stages(s_odd, mx_odd, s_even, mx_even, e1_even, e2_even, l_even, e1_odd, e2_odd, l_odd)


def _diff_attention(proj3, lam_vecs, tab, subln_g, lam_init):
    b, seq, _ = proj3.shape
    nq = seq // TQ
    n_steps = b * H_DIFF * nq

    def split(t):
        return t // (H_DIFF * nq), (t // nq) % H_DIFF, t % nq

    def cur(col0, whole):
        def index(t):
            bi, h, i = split(jnp.minimum(t, n_steps - 1))
            return bi, (0 if whole else i), col0 + h
        return index

    def prev(col0, whole):
        def index(t):
            bi, h, i = split(jnp.clip(t - 2, 0, n_steps - 1))
            return bi, (0 if whole else i), col0 + h
        return index

    s_shape = pltpu.VMEM((2 * seq, TQ), _F32)
    e_shape = pltpu.VMEM((seq, TQ), _BF16)
    stat_shape = pltpu.VMEM((2, SUBLANES, TQ), _F32)
    return pl.pallas_call(
        lambda *refs: _diff_kernel(*refs, seq=seq, n_steps=n_steps, lam_init=lam_init),
        grid=(n_steps + 2,),
        in_specs=[
            pl.BlockSpec(lam_vecs.shape, lambda t: (0, 0)),
            pl.BlockSpec((None, TQ, DV_DIFF), cur(0, False)),
            pl.BlockSpec((None, seq, DV_DIFF), cur(H_DIFF, True)),
            pl.BlockSpec((None, seq, DV_DIFF), prev(2 * H_DIFF, True)),
            pl.BlockSpec((None, TQ, DV_DIFF), prev(3 * H_DIFF, False)),
            pl.BlockSpec(tab.shape, lambda t: (0, 0, 0)),
            pl.BlockSpec((1, DV_DIFF), lambda t: (0, 0)),
        ],
        out_specs=pl.BlockSpec((None, TQ, DV_DIFF), prev(0, False)),
        out_shape=jax.ShapeDtypeStruct((b, seq, W_A), _BF16),
        scratch_shapes=[
            pltpu.VMEM((2 * seq, 2 * KEY_BLK), _BF16),
            pltpu.VMEM((DV_DIFF, seq), _BF16),
            s_shape, s_shape, stat_shape, stat_shape, stat_shape, stat_shape,
            e_shape, e_shape, e_shape, e_shape,
        ],
        compiler_params=pltpu.CompilerParams(
            dimension_semantics=("arbitrary",), vmem_limit_bytes=VMEM_LIMIT),
        name="diff_attn",
    )(lam_vecs, proj3, proj3, proj3, proj3, tab, subln_g)


def _nbr_kernel(q_ref, k_ref, v_ref, z_ref, bm_ref, o_ref, ke_scr, ko_scr, *, rows, kr):
    g = pl.program_id(1)
    pair_w = 2 * DH_NA

    @pl.when(g == 0)
    def _():
        k = k_ref[...]
        lane = lax.broadcasted_iota(jnp.int32, k.shape, 1) & (pair_w - 1)
        zero = jnp.zeros_like(k)
        ke_scr[...] = jnp.where(lane < DH_NA, k, zero)
        ko_scr[...] = jnp.where(lane >= DH_NA, k, zero)

    nt = (((1,), (1,)), ((), ()))
    win = NA_KROWS * GRID_W
    tok = NA_ROWS * GRID_W
    ws = jnp.clip(g * NA_ROWS - kr // 2, 0, rows - NA_KROWS)
    start = pl.multiple_of(ws * GRID_W, GRID_W)
    out_lane = lax.broadcasted_iota(jnp.int32, (tok, pair_w), 1)
    probs = []
    for h in range(H_NA):
        csl = slice((h // 2) * pair_w, (h // 2 + 1) * pair_w)
        kw = (ke_scr, ko_scr)[h % 2][pl.ds(start, win), csl]
        s = lax.dot_general(q_ref[:, csl], kw, nt, preferred_element_type=_F32)
        s = s + bm_ref[h]
        e = jnp.exp2(s - jnp.max(s, axis=-1, keepdims=True))
        probs.append((e.astype(_BF16), 1.0 / jnp.sum(e, axis=-1, keepdims=True)))
    for hp in range(H_NA // 2):
        csl = slice(hp * pair_w, (hp + 1) * pair_w)
        vw = v_ref[pl.ds(start, win), csl]
        halves = [jnp.dot(e, vw, preferred_element_type=_F32) * inv
                  for e, inv in probs[2 * hp:2 * hp + 2]]
        o = jnp.where(out_lane < DH_NA, halves[0], halves[1])
        o_ref[:, csl] = (o * z_ref[:, csl].astype(_F32)).astype(_BF16)


def _nbr_attention(proj3, bm_tab, kr):
    b, seq, _ = proj3.shape
    rows = seq // GRID_W
    n_grp = rows // NA_ROWS
    tok = NA_ROWS * GRID_W
    sig = lambda g: tuple(
        (0 <= _na_window_start(g, rows, kr) + ko - _na_first_key_row(g * NA_ROWS + rr, rows, kr) < kr,
         _na_window_start(g, rows, kr) + ko - (g * NA_ROWS + rr))
        for rr in range(NA_ROWS) for ko in range(NA_KROWS))
    variant_of = lambda g: (g > 0) + (g == n_grp - 1)
    reps = _na_variant_groups(rows)
    assert all(sig(g) == sig(reps[variant_of(g)]) for g in range(n_grp))
    grp = lambda gi: (lambda bi, g: (bi, g, gi))
    full = lambda gi: (lambda bi, g: (bi, 0, gi))
    return pl.pallas_call(
        lambda *refs: _nbr_kernel(*refs, rows=rows, kr=kr),
        grid=(b, n_grp),
        in_specs=[
            pl.BlockSpec((None, tok, W_B), grp(4)),
            pl.BlockSpec((None, seq, W_B), full(5)),
            pl.BlockSpec((None, seq, W_B), full(6)),
            pl.BlockSpec((None, tok, W_B), grp(7)),
            pl.BlockSpec((None,) + bm_tab.shape[1:],
                         lambda bi, g: ((g > 0).astype(jnp.int32) + (g == n_grp - 1).astype(jnp.int32),
                                        0, 0, 0)),
        ],
        out_specs=pl.BlockSpec((None, tok, W_B), lambda bi, g: (bi, g, 0)),
        out_shape=jax.ShapeDtypeStruct((b, seq, W_B), _BF16),
        scratch_shapes=[pltpu.VMEM((seq, W_B), _BF16), pltpu.VMEM((seq, W_B), _BF16)],
        compiler_params=pltpu.CompilerParams(
            dimension_semantics=("arbitrary", "arbitrary"), vmem_limit_bytes=VMEM_LIMIT),
        name="nbr_attn",
    )(proj3, proj3, proj3, proj3, bm_tab)


def _out_kernel(x_ref, ya_ref, yb_ref, p_ref, wo_ref, wg_ref, wp_ref, o_ref):
    y = (jnp.dot(ya_ref[...], wo_ref[:W_A, :], preferred_element_type=_F32)
         + jnp.dot(yb_ref[...], wo_ref[W_A:, :], preferred_element_type=_F32))
    x1 = x_ref[...] + y
    gate = jax.nn.sigmoid(jnp.dot(x1.astype(_BF16), wg_ref[...], preferred_element_type=_F32))
    ple = jnp.dot(p_ref[...].astype(_BF16), wp_ref[...], preferred_element_type=_F32)
    o_ref[...] = x1 + gate * ple


def _out(x2, ya2, yb2, p2, wo, wg, wp):
    n = x2.shape[0]
    tile = lambda w: pl.BlockSpec((TM_OUT, w), lambda i: (i, 0))
    whole = lambda a: pl.BlockSpec(a.shape, lambda i: (0, 0))
    return pl.pallas_call(
        _out_kernel,
        grid=(n // TM_OUT,),
        in_specs=[tile(D_MODEL), tile(W_A), tile(W_B), tile(PLE_DIM),
                  whole(wo), whole(wg), whole(wp)],
        out_specs=tile(D_MODEL),
        out_shape=jax.ShapeDtypeStruct((n, D_MODEL), _F32),
        compiler_params=pltpu.CompilerParams(
            dimension_semantics=("arbitrary",), vmem_limit_bytes=VMEM_LIMIT),
        name="out_proj",
    )(x2, ya2, yb2, p2, wo, wg, wp)


def _segment_ones():
    idx = np.arange(GROUP_W // 2) // HEAD_DIM
    return jnp.asarray((idx[:, None] == idx[None, :]).astype(np.float32), dtype=_BF16)


def kernel(x, p, norm_g, w_in, w_out, q_norm_a, k_norm_a, lam_q1, lam_k1, lam_q2, lam_k2,
           subln_g, t5_bias, q_norm_b, k_norm_b, na_rpb, w_ple_gate, w_ple_proj):
    b, seq, _ = x.shape
    depth = w_in.shape[0]
    n = b * seq
    rows = seq // GRID_W
    kr = min(WIN_R, rows)
    assert kr == WIN_R and rows % NA_ROWS == 0 and seq % TQ == 0 and n % TM_PROJ == 0

    seg = _segment_ones()
    bias_tab = _t5_table(t5_bias.astype(_F32), seq)
    ones_w = jnp.ones((1, GROUP_W), _F32)
    for i in range(depth):
        lam_init = 0.8 - 0.6 * math.exp(-0.3 * i)
        tile_a = lambda g, s: jnp.tile(g.astype(_F32), GROUP_W // DH_DIFF)[None, :] * s
        gains = jnp.concatenate([
            tile_a(q_norm_a[i], DH_DIFF ** -0.5 * LOG2E), tile_a(k_norm_a[i], 1.0), ones_w, ones_w,
            tile_a(q_norm_b[i], DH_NA ** -0.5 * LOG2E), tile_a(k_norm_b[i], 1.0), ones_w, ones_w],
            axis=-1)
        proj = _proj(x.reshape(n, D_MODEL), norm_g[i][None, :].astype(_F32),
                     w_in[i].astype(_BF16), seg, gains)
        proj3 = proj.reshape(b, seq, D_IN)
        lam_vecs = jnp.stack([lam_q1[i], lam_k1[i], lam_q2[i], lam_k2[i]]).astype(_F32)
        ya = _diff_attention(proj3, lam_vecs, bias_tab, subln_g[i][None, :].astype(_F32), lam_init)
        bm_tab = _na_table(na_rpb[i].astype(_F32), rows, kr)
        yb = _nbr_attention(proj3, bm_tab, kr)
        x = _out(x.reshape(n, D_MODEL), ya.reshape(n, W_A), yb.reshape(n, W_B),
                 p[i].reshape(n, PLE_DIM), w_out[i].astype(_BF16),
                 w_ple_gate[i].astype(_BF16), w_ple_proj[i].astype(_BF16)).reshape(b, seq, D_MODEL)
    return x
```

```python
import math

import jax
import jax.numpy as jnp
import numpy as np
from jax import lax
from jax.experimental import pallas as pl
from jax.experimental.pallas import tpu as pltpu

D_MODEL = 1024
PLE_DIM = 256
GRID_W = 64
H_DIFF = 4
DH_DIFF = 64
DV_DIFF = 2 * DH_DIFF
H_NA = 8
DH_NA = 64
WIN_R = 8
WIN_C = 16
N_BUCKETS = 32
MAX_DIST = 128
W_A = H_DIFF * DV_DIFF
W_B = H_NA * DH_NA
D_IN = 4 * W_A + 4 * W_B
EPS = 1e-6
NEG = -1e30

GROUP_W = 512
NORM_GROUPS = (0, 1, 4, 5)
GATE_GROUPS = (3, 7)
HEAD_DIM = 64

TM_PROJ = 512
TM_OUT = 512
CAST_ROWS = 128
TQ = 512
KEY_BLK = 128
SUBLANES = 8
NA_ROWS = 4
NA_KROWS = NA_ROWS + WIN_R
NA_LEAD = 2
LOG2E = math.log2(math.e)
VMEM_LIMIT = 56 * 1024 * 1024

_F32 = jnp.float32
_BF16 = jnp.bfloat16


def _t5_thresholds():
    half = N_BUCKETS // 2
    max_exact = half // 2
    steps = half - max_exact
    ratio = MAX_DIST // max_exact
    thr = []
    for j in range(1, steps):
        n = max_exact
        while n ** steps < (max_exact ** steps) * (ratio ** j):
            n += 1
        thr.append(n)
    return tuple(thr)


def _t5_table_rows(seq):
    return 2 * seq - KEY_BLK


def _t5_table_kernel(tb_ref, o_ref, *, seq):
    h = pl.program_id(0)
    half = N_BUCKETS // 2
    max_exact = half // 2
    blk = (KEY_BLK, KEY_BLK)
    far_pos = tb_ref[(N_BUCKETS - 1) * H_DIFF + h] * LOG2E
    far_neg = tb_ref[(half - 1) * H_DIFF + h] * LOG2E
    for rb in range(_t5_table_rows(seq) // KEY_BLK):
        d_lo = KEY_BLK * rb - (seq - KEY_BLK)
        if -(d_lo + KEY_BLK - 1) >= MAX_DIST:
            val = jnp.full(blk, far_pos, _F32)
        elif (KEY_BLK - 1) - d_lo <= -MAX_DIST:
            val = jnp.full(blk, far_neg, _F32)
        else:
            d = lax.broadcasted_iota(jnp.int32, blk, 0) + d_lo
            rel = lax.broadcasted_iota(jnp.int32, blk, 1) - d
            n = jnp.abs(rel)
            large = jnp.full_like(n, max_exact)
            for thr in _t5_thresholds():
                large = large + (n >= thr).astype(jnp.int32)
            bucket = jnp.where(rel > 0, half, 0) + jnp.where(n < max_exact, n, large)
            val = jnp.zeros(blk, _F32)
            for j in range(N_BUCKETS):
                val = jnp.where(bucket == j, tb_ref[j * H_DIFF + h] * LOG2E, val)
        o_ref[rb * KEY_BLK:(rb + 1) * KEY_BLK, :] = val.astype(_BF16)


def _t5_table(t5_bias, seq):
    rows = _t5_table_rows(seq)
    return pl.pallas_call(
        lambda tb, o: _t5_table_kernel(tb, o, seq=seq),
        grid=(H_DIFF,),
        in_specs=[pl.BlockSpec(memory_space=pltpu.SMEM)],
        out_specs=pl.BlockSpec((None, rows, KEY_BLK), lambda h: (h, 0, 0)),
        out_shape=jax.ShapeDtypeStruct((H_DIFF, rows, KEY_BLK), _BF16),
        name="t5_table",
    )(t5_bias.reshape(-1))


def _na_first_key_row(r, rows, kr):
    return min(max(r - kr // 2, 0), rows - kr)


def _na_window_start(g, rows, kr):
    return min(max(g * NA_ROWS - kr // 2, 0), rows - NA_KROWS)


def _na_variant_groups(rows):
    return (0, 1, rows // NA_ROWS - 1)


def _na_table_kernel(rpb_ref, o_ref, *, rows, kr):
    h = pl.program_id(0)
    n_dr = 2 * WIN_R - 1
    n_dc = 2 * WIN_C - 1
    pair = (GRID_W, 2 * GRID_W)
    cq = lax.broadcasted_iota(jnp.int32, pair, 0)
    lane = lax.broadcasted_iota(jnp.int32, pair, 1)
    ck = lane & (GRID_W - 1)
    dc = jnp.clip(ck - cq, -(WIN_C - 1), WIN_C - 1) + (WIN_C - 1)
    cs = jnp.clip(cq - WIN_C // 2, 0, GRID_W - WIN_C)
    col_ok = (ck >= cs) & (ck < cs + WIN_C)
    left = lane < GRID_W
    neg = jnp.full(pair, NEG, _F32)
    tiles = {}

    def tile(dr):
        if dr not in tiles:
            acc = jnp.zeros(pair, _F32)
            for j in range(n_dc):
                acc = jnp.where(dc == j, rpb_ref[(h * n_dr + dr) * n_dc + j] * LOG2E, acc)
            tiles[dr] = jnp.where(col_ok, acc, NEG)
        return tiles[dr]

    for v, g in enumerate(_na_variant_groups(rows)):
        ws = _na_window_start(g, rows, kr)
        for rr in range(NA_ROWS):
            r = g * NA_ROWS + rr
            rs = _na_first_key_row(r, rows, kr)
            for kp in range(NA_KROWS // 2):
                parts = []
                for ko in (2 * kp, 2 * kp + 1):
                    in_win = 0 <= ws + ko - rs < kr
                    parts.append(tile(ws + ko - r + WIN_R - 1) if in_win else neg)
                val = parts[0] if parts[0] is parts[1] else jnp.where(left, parts[0], parts[1])
                o_ref[v, rr * GRID_W:(rr + 1) * GRID_W, kp * 2 * GRID_W:(kp + 1) * 2 * GRID_W] = val


def _na_table(rpb, rows, kr):
    n_var = len(_na_variant_groups(rows))
    shape = (n_var, H_NA, NA_ROWS * GRID_W, NA_KROWS * GRID_W)
    return pl.pallas_call(
        lambda r, o: _na_table_kernel(r, o, rows=rows, kr=kr),
        grid=(H_NA,),
        in_specs=[pl.BlockSpec(memory_space=pltpu.SMEM)],
        out_specs=pl.BlockSpec((n_var, None) + shape[2:], lambda h: (0, h, 0, 0)),
        out_shape=jax.ShapeDtypeStruct(shape, _F32),
        name="na_table",
    )(rpb.reshape(-1))


def _cast_weights_once(w_ref, w_scr):
    @pl.when(pl.program_id(0) == 0)
    def _():
        rows = w_ref.shape[0]
        for r0 in range(0, rows, CAST_ROWS):
            w_scr[r0:r0 + CAST_ROWS, :] = w_ref[r0:r0 + CAST_ROWS, :].astype(_BF16)


def _proj_kernel(x_ref, ng_ref, w_ref, seg_ref, gain_ref, o_ref, w_scr):
    _cast_weights_once(w_ref, w_scr)
    x = x_ref[...]
    ms = jnp.mean(x * x, axis=-1, keepdims=True)
    xn = (x * lax.rsqrt(ms + EPS) * ng_ref[...]).astype(_BF16)
    seg = seg_ref[...]
    half_w = GROUP_W // 2
    for j in range(D_IN // GROUP_W):
        sl = slice(j * GROUP_W, (j + 1) * GROUP_W)
        acc = jnp.dot(xn, w_scr[:, sl], preferred_element_type=_F32)
        if j in NORM_GROUPS:
            sq = (acc * acc).astype(_BF16)
            ss = jnp.concatenate(
                [jnp.dot(sq[:, :half_w], seg, preferred_element_type=_F32),
                 jnp.dot(sq[:, half_w:], seg, preferred_element_type=_F32)], axis=-1)
            acc = acc * lax.rsqrt(ss * (1.0 / HEAD_DIM) + EPS) * gain_ref[:, sl]
        elif j in GATE_GROUPS:
            acc = acc * jax.nn.sigmoid(acc)
        o_ref[:, sl] = acc.astype(_BF16)


def _resident(shape):
    return pl.BlockSpec(shape, lambda i: (0,) * len(shape), pipeline_mode=pl.Buffered(1))


def _proj(x2, norm_g, w, seg, gains):
    n = x2.shape[0]
    return pl.pallas_call(
        _proj_kernel,
        grid=(n // TM_PROJ,),
        in_specs=[
            pl.BlockSpec((TM_PROJ, D_MODEL), lambda i: (i, 0)),
            pl.BlockSpec((1, D_MODEL), lambda i: (0, 0)),
            _resident((D_MODEL, D_IN)),
            pl.BlockSpec(seg.shape, lambda i: (0, 0)),
            pl.BlockSpec((1, D_IN), lambda i: (0, 0)),
        ],
        out_specs=pl.BlockSpec((TM_PROJ, D_IN), lambda i: (i, 0)),
        out_shape=jax.ShapeDtypeStruct((n, D_IN), _BF16),
        scratch_shapes=[pltpu.VMEM((D_MODEL, D_IN), _BF16)],
        compiler_params=pltpu.CompilerParams(
            dimension_semantics=("arbitrary",), vmem_limit_bytes=VMEM_LIMIT),
        name="proj",
    )(x2, norm_g, w, seg, gains)


def _fold_rows(x, op):
    parts = [x[i * SUBLANES:(i + 1) * SUBLANES] for i in range(x.shape[0] // SUBLANES)]
    while len(parts) > 1:
        parts = [op(parts[i], parts[i + 1]) if i + 1 < len(parts) else parts[i]
                 for i in range(0, len(parts), 2)]
    return parts[0]


def _diff_kernel(lam_ref, q_ref, k_ref, v_ref, z_ref, tab_ref, g_ref, o_ref,
                 kaug_scr, vt_scr, s_even, s_odd, mx_even, mx_odd, l_even, l_odd,
                 e1_even, e2_even, e1_odd, e2_odd, *, seq, n_steps, lam_init):
    nq = seq // TQ
    nkb = seq // KEY_BLK
    t = pl.program_id(0)
    t_a = jnp.minimum(t, n_steps - 1)
    i_a = t_a % nq
    h_a = (t_a // nq) % H_DIFF
    i_c = jnp.clip(t - 2, 0, n_steps - 1) % nq
    nt = (((1,), (1,)), ((), ()))

    @pl.when(t == 0)
    def _():
        row = lax.broadcasted_iota(jnp.int32, (KEY_BLK, KEY_BLK), 0)
        col = lax.broadcasted_iota(jnp.int32, (KEY_BLK, KEY_BLK), 1)
        eye = (row == col).astype(_BF16)
        for j in range(2 * nkb):
            kaug_scr[j * KEY_BLK:(j + 1) * KEY_BLK, KEY_BLK:] = eye
        s_odd[...] = jnp.zeros_like(s_odd)
        mx_odd[...] = jnp.zeros_like(mx_odd)
        e1_even[...] = jnp.zeros_like(e1_even)
        e2_even[...] = jnp.zeros_like(e2_even)
        l_even[...] = jnp.ones_like(l_even)

    @pl.when(i_a == 0)
    def _():
        lane = lax.broadcasted_iota(jnp.int32, (KEY_BLK, DV_DIFF), 1)
        zero = jnp.zeros((KEY_BLK, DV_DIFF), _BF16)
        for j in range(nkb):
            kj = k_ref[j * KEY_BLK:(j + 1) * KEY_BLK, :]
            kaug_scr[2 * j * KEY_BLK:(2 * j + 1) * KEY_BLK, :KEY_BLK] = jnp.where(lane < DH_DIFF, kj, zero)
            kaug_scr[(2 * j + 1) * KEY_BLK:(2 * j + 2) * KEY_BLK, :KEY_BLK] = jnp.where(lane >= DH_DIFF, kj, zero)

    @pl.when(i_c == 0)
    def _():
        for j in range(nkb):
            vj = v_ref[j * KEY_BLK:(j + 1) * KEY_BLK, :].astype(_F32)
            vt_scr[:, j * KEY_BLK:(j + 1) * KEY_BLK] = vj.T.astype(_BF16)

    def stages(s_w, mx_w, s_r, mx_r, e1_w, e2_w, l_w, e1_r, e2_r, l_r):
        q = q_ref[...]

        def score_block(j, mx):
            off = pl.multiple_of(i_a * TQ + (seq - KEY_BLK - KEY_BLK * j), KEY_BLK)
            qa = jnp.concatenate([q, tab_ref[h_a, pl.ds(off, TQ), :]], axis=1)
            ka = kaug_scr[2 * j * KEY_BLK:(2 * j + 2) * KEY_BLK, :]
            sj = lax.dot_general(ka, qa, nt, preferred_element_type=_F32)
            s_w[2 * j * KEY_BLK:(2 * j + 2) * KEY_BLK, :] = sj
            p1 = _fold_rows(sj[:KEY_BLK], jnp.maximum)
            p2 = _fold_rows(sj[KEY_BLK:], jnp.maximum)
            return (p1, p2) if mx is None else (jnp.maximum(mx[0], p1), jnp.maximum(mx[1], p2))

        m1 = jnp.max(mx_r[0], axis=0, keepdims=True)
        m2 = jnp.max(mx_r[1], axis=0, keepdims=True)

        def exp_block(j, ls):
            e1 = jnp.exp2(s_r[2 * j * KEY_BLK:(2 * j + 1) * KEY_BLK, :] - m1)
            e2 = jnp.exp2(s_r[(2 * j + 1) * KEY_BLK:(2 * j + 2) * KEY_BLK, :] - m2)
            e1_w[j * KEY_BLK:(j + 1) * KEY_BLK, :] = e1.astype(_BF16)
            e2_w[j * KEY_BLK:(j + 1) * KEY_BLK, :] = e2.astype(_BF16)
            p1 = _fold_rows(e1, jnp.add)
            p2 = _fold_rows(e2, jnp.add)
            return (p1, p2) if ls is None else (ls[0] + p1, ls[1] + p2)

        def pv_epilogue():
            lv = lam_ref[...]
            lam = (jnp.exp(jnp.sum(lv[0:1] * lv[1:2], axis=-1, keepdims=True))
                   - jnp.exp(jnp.sum(lv[2:3] * lv[3:4], axis=-1, keepdims=True)) + lam_init)
            l1 = jnp.sum(l_r[0], axis=0, keepdims=True)
            l2 = jnp.sum(l_r[1], axis=0, keepdims=True)
            r = (lam * l1 / l2).astype(_BF16)
            a_t = e1_r[...] - r * e2_r[...]
            o_t = jnp.dot(vt_scr[...], a_t, preferred_element_type=_F32) * (1.0 / l1)
            ms = jnp.mean(o_t * o_t, axis=0, keepdims=True)
            o = (o_t * lax.rsqrt(ms + EPS)).T
            o = o * g_ref[...] * (1.0 - lam_init)
            o_ref[...] = (o * z_ref[...].astype(_F32)).astype(_BF16)

        pv_epilogue()
        mx, ls = None, None
        for j in range(nkb):
            ls = exp_block(j, ls)
            mx = score_block(j, mx)
        mx_w[0], mx_w[1] = mx
        l_w[0], l_w[1] = ls

    @pl.when(t % 2 == 0)
    def _():
        stages(s_even, mx_even, s_odd, mx_odd, e1_odd, e2_odd, l_odd, e1_even, e2_even, l_even)

    @pl.when(t % 2 == 1)
    def _():
        stages(s_odd, mx_odd, s_even, mx_even, e1_even, e2_even, l_even, e1_odd, e2_odd, l_odd)


def _diff_attention(proj3, lam_vecs, tab, subln_g, lam_init):
    b, seq, _ = proj3.shape
    nq = seq // TQ
    n_steps = b * H_DIFF * nq

    def split(t):
        return t // (H_DIFF * nq), (t // nq) % H_DIFF, t % nq

    def cur(col0, whole):
        def index(t):
            bi, h, i = split(jnp.minimum(t, n_steps - 1))
            return bi, (0 if whole else i), col0 + h
        return index

    def prev(col0, whole):
        def index(t):
            bi, h, i = split(jnp.clip(t - 2, 0, n_steps - 1))
            return bi, (0 if whole else i), col0 + h
        return index

    s_shape = pltpu.VMEM((2 * seq, TQ), _F32)
    e_shape = pltpu.VMEM((seq, TQ), _BF16)
    stat_shape = pltpu.VMEM((2, SUBLANES, TQ), _F32)
    return pl.pallas_call(
        lambda *refs: _diff_kernel(*refs, seq=seq, n_steps=n_steps, lam_init=lam_init),
        grid=(n_steps + 2,),
        in_specs=[
            pl.BlockSpec(lam_vecs.shape, lambda t: (0, 0)),
            pl.BlockSpec((None, TQ, DV_DIFF), cur(0, False)),
            pl.BlockSpec((None, seq, DV_DIFF), cur(H_DIFF, True)),
            pl.BlockSpec((None, seq, DV_DIFF), prev(2 * H_DIFF, True)),
            pl.BlockSpec((None, TQ, DV_DIFF), prev(3 * H_DIFF, False)),
            pl.BlockSpec(tab.shape, lambda t: (0, 0, 0)),
            pl.BlockSpec((1, DV_DIFF), lambda t: (0, 0)),
        ],
        out_specs=pl.BlockSpec((None, TQ, DV_DIFF), prev(0, False)),
        out_shape=jax.ShapeDtypeStruct((b, seq, W_A), _BF16),
        scratch_shapes=[
            pltpu.VMEM((2 * seq, 2 * KEY_BLK), _BF16),
            pltpu.VMEM((DV_DIFF, seq), _BF16),
            s_shape, s_shape, stat_shape, stat_shape, stat_shape, stat_shape,
            e_shape, e_shape, e_shape, e_shape,
        ],
        compiler_params=pltpu.CompilerParams(
            dimension_semantics=("arbitrary",), vmem_limit_bytes=VMEM_LIMIT),
        name="diff_attn",
    )(lam_vecs, proj3, proj3, proj3, proj3, tab, subln_g)


def _nbr_kernel(q_ref, k_ref, v_ref, z_ref, bm_ref, o_ref, ke_scr, ko_scr, *, rows, kr):
    g = pl.program_id(1)
    pair_w = 2 * DH_NA

    @pl.when(g == 0)
    def _():
        k = k_ref[...]
        lane = lax.broadcasted_iota(jnp.int32, k.shape, 1) & (pair_w - 1)
        zero = jnp.zeros_like(k)
        ke_scr[...] = jnp.where(lane < DH_NA, k, zero)
        ko_scr[...] = jnp.where(lane >= DH_NA, k, zero)

    nt = (((1,), (1,)), ((), ()))
    win = NA_KROWS * GRID_W
    tok = NA_ROWS * GRID_W
    ws = jnp.clip(g * NA_ROWS - kr // 2, 0, rows - NA_KROWS)
    start = pl.multiple_of(ws * GRID_W, GRID_W)
    out_lane = lax.broadcasted_iota(jnp.int32, (tok, pair_w), 1)

    def scores(h):
        csl = slice((h // 2) * pair_w, (h // 2 + 1) * pair_w)
        kw = (ke_scr, ko_scr)[h % 2][pl.ds(start, win), csl]
        s = lax.dot_general(q_ref[:, csl], kw, nt, preferred_element_type=_F32)
        s = s + bm_ref[h]
        e = jnp.exp2(s - jnp.max(s, axis=-1, keepdims=True))
        return e.astype(_BF16), 1.0 / jnp.sum(e, axis=-1, keepdims=True)

    def weighted_values(h, prob):
        e, inv = prob
        csl = slice((h // 2) * pair_w, (h // 2 + 1) * pair_w)
        return jnp.dot(e, v_ref[pl.ds(start, win), csl], preferred_element_type=_F32) * inv

    probs, outs = {}, {}
    for step in range(H_NA + NA_LEAD):
        if step < H_NA:
            probs[step] = scores(step)
        h = step - NA_LEAD
        if h >= 0:
            outs[h] = weighted_values(h, probs.pop(h))
            if h % 2 == 1:
                csl = slice((h // 2) * pair_w, (h // 2 + 1) * pair_w)
                o = jnp.where(out_lane < DH_NA, outs.pop(h - 1), outs.pop(h))
                o_ref[:, csl] = (o * z_ref[:, csl].astype(_F32)).astype(_BF16)


def _nbr_attention(proj3, bm_tab, kr):
    b, seq, _ = proj3.shape
    rows = seq // GRID_W
    n_grp = rows // NA_ROWS
    tok = NA_ROWS * GRID_W
    sig = lambda g: tuple(
        (0 <= _na_window_start(g, rows, kr) + ko - _na_first_key_row(g * NA_ROWS + rr, rows, kr) < kr,
         _na_window_start(g, rows, kr) + ko - (g * NA_ROWS + rr))
        for rr in range(NA_ROWS) for ko in range(NA_KROWS))
    variant_of = lambda g: (g > 0) + (g == n_grp - 1)
    reps = _na_variant_groups(rows)
    assert all(sig(g) == sig(reps[variant_of(g)]) for g in range(n_grp))
    grp = lambda gi: (lambda bi, g: (bi, g, gi))
    full = lambda gi: (lambda bi, g: (bi, 0, gi))
    return pl.pallas_call(
        lambda *refs: _nbr_kernel(*refs, rows=rows, kr=kr),
        grid=(b, n_grp),
        in_specs=[
            pl.BlockSpec((None, tok, W_B), grp(4)),
            pl.BlockSpec((None, seq, W_B), full(5)),
            pl.BlockSpec((None, seq, W_B), full(6)),
            pl.BlockSpec((None, tok, W_B), grp(7)),
            pl.BlockSpec((None,) + bm_tab.shape[1:],
                         lambda bi, g: ((g > 0).astype(jnp.int32) + (g == n_grp - 1).astype(jnp.int32),
                                        0, 0, 0)),
        ],
        out_specs=pl.BlockSpec((None, tok, W_B), lambda bi, g: (bi, g, 0)),
        out_shape=jax.ShapeDtypeStruct((b, seq, W_B), _BF16),
        scratch_shapes=[pltpu.VMEM((seq, W_B), _BF16), pltpu.VMEM((seq, W_B), _BF16)],
        compiler_params=pltpu.CompilerParams(
            dimension_semantics=("arbitrary", "arbitrary"), vmem_limit_bytes=VMEM_LIMIT),
        name="nbr_attn",
    )(proj3, proj3, proj3, proj3, bm_tab)


def _out_kernel(x_ref, ya_ref, yb_ref, p_ref, wo_ref, wg_ref, wp_ref, o_ref,
                wo_scr, wg_scr, wp_scr):
    _cast_weights_once(wo_ref, wo_scr)
    _cast_weights_once(wg_ref, wg_scr)
    _cast_weights_once(wp_ref, wp_scr)
    y = (jnp.dot(ya_ref[...], wo_scr[:W_A, :], preferred_element_type=_F32)
         + jnp.dot(yb_ref[...], wo_scr[W_A:, :], preferred_element_type=_F32))
    x1 = x_ref[...] + y
    gate = jax.nn.sigmoid(jnp.dot(x1.astype(_BF16), wg_scr[...], preferred_element_type=_F32))
    ple = jnp.dot(p_ref[...].astype(_BF16), wp_scr[...], preferred_element_type=_F32)
    o_ref[...] = x1 + gate * ple


def _out(x2, ya2, yb2, p2, wo, wg, wp):
    n = x2.shape[0]
    tile = lambda w: pl.BlockSpec((TM_OUT, w), lambda i: (i, 0))
    return pl.pallas_call(
        _out_kernel,
        grid=(n // TM_OUT,),
        in_specs=[tile(D_MODEL), tile(W_A), tile(W_B), tile(PLE_DIM),
                  _resident(wo.shape), _resident(wg.shape), _resident(wp.shape)],
        out_specs=tile(D_MODEL),
        out_shape=jax.ShapeDtypeStruct((n, D_MODEL), _F32),
        scratch_shapes=[pltpu.VMEM(w.shape, _BF16) for w in (wo, wg, wp)],
        compiler_params=pltpu.CompilerParams(
            dimension_semantics=("arbitrary",), vmem_limit_bytes=VMEM_LIMIT),
        name="out_proj",
    )(x2, ya2, yb2, p2, wo, wg, wp)


def _segment_ones():
    idx = np.arange(GROUP_W // 2) // HEAD_DIM
    return jnp.asarray((idx[:, None] == idx[None, :]).astype(np.float32), dtype=_BF16)


def kernel(x, p, norm_g, w_in, w_out, q_norm_a, k_norm_a, lam_q1, lam_k1, lam_q2, lam_k2,
           subln_g, t5_bias, q_norm_b, k_norm_b, na_rpb, w_ple_gate, w_ple_proj):
    b, seq, _ = x.shape
    depth = w_in.shape[0]
    n = b * seq
    rows = seq // GRID_W
    kr = min(WIN_R, rows)
    assert kr == WIN_R and rows % NA_ROWS == 0 and seq % TQ == 0 and n % TM_PROJ == 0

    seg = _segment_ones()
    bias_tab = _t5_table(t5_bias.astype(_F32), seq)
    ones_w = jnp.ones((1, GROUP_W), _F32)
    for i in range(depth):
        lam_init = 0.8 - 0.6 * math.exp(-0.3 * i)
        tile_a = lambda g, s: jnp.tile(g.astype(_F32), GROUP_W // DH_DIFF)[None, :] * s
        gains = jnp.concatenate([
            tile_a(q_norm_a[i], DH_DIFF ** -0.5 * LOG2E), tile_a(k_norm_a[i], 1.0), ones_w, ones_w,
            tile_a(q_norm_b[i], DH_NA ** -0.5 * LOG2E), tile_a(k_norm_b[i], 1.0), ones_w, ones_w],
            axis=-1)
        proj = _proj(x.reshape(n, D_MODEL), norm_g[i][None, :].astype(_F32),
                     w_in[i].astype(_F32), seg, gains)
        proj3 = proj.reshape(b, seq, D_IN)
        lam_vecs = jnp.stack([lam_q1[i], lam_k1[i], lam_q2[i], lam_k2[i]]).astype(_F32)
        ya = _diff_attention(proj3, lam_vecs, bias_tab, subln_g[i][None, :].astype(_F32), lam_init)
        bm_tab = _na_table(na_rpb[i].astype(_F32), rows, kr)
        yb = _nbr_attention(proj3, bm_tab, kr)
        x = _out(x.reshape(n, D_MODEL), ya.reshape(n, W_A), yb.reshape(n, W_B),
                 p[i].reshape(n, PLE_DIM), w_out[i].astype(_F32),
                 w_ple_gate[i].astype(_F32), w_ple_proj[i].astype(_F32)).reshape(b, seq, D_MODEL)
    return x
```

```python
import math

import jax
import jax.numpy as jnp
import numpy as np
from jax import lax
from jax.experimental import pallas as pl
from jax.experimental.pallas import tpu as pltpu

D_MODEL = 1024
PLE_DIM = 256
GRID_W = 64
H_DIFF = 4
DH_DIFF = 64
DV_DIFF = 2 * DH_DIFF
H_NA = 8
DH_NA = 64
WIN_R = 8
WIN_C = 16
N_BUCKETS = 32
MAX_DIST = 128
W_A = H_DIFF * DV_DIFF
W_B = H_NA * DH_NA
D_IN = 4 * W_A + 4 * W_B
EPS = 1e-6
NEG = -1e30

GROUP_W = 512
NORM_GROUPS = (0, 1, 4, 5)
GATE_GROUPS = (3, 7)
KA_GROUP, VA_GROUP = 1, 2
HEAD_DIM = 64

TM_PROJ = 512
TM_OUT = 512
CAST_ROWS = 128
TQ = 512
KEY_BLK = 128
PV_CHUNK = 256
SUBLANES = 8
NA_ROWS = 4
NA_KROWS = NA_ROWS + WIN_R
NA_LEAD = 2
LOG2E = math.log2(math.e)
VMEM_LIMIT = 56 * 1024 * 1024

_F32 = jnp.float32
_BF16 = jnp.bfloat16


def _t5_thresholds():
    half = N_BUCKETS // 2
    max_exact = half // 2
    steps = half - max_exact
    ratio = MAX_DIST // max_exact
    thr = []
    for j in range(1, steps):
        n = max_exact
        while n ** steps < (max_exact ** steps) * (ratio ** j):
            n += 1
        thr.append(n)
    return tuple(thr)


def _t5_table_rows(seq):
    return 2 * seq - KEY_BLK


def _t5_table_kernel(tb_ref, o_ref, *, seq):
    h = pl.program_id(0)
    half = N_BUCKETS // 2
    max_exact = half // 2
    blk = (KEY_BLK, KEY_BLK)
    far_pos = tb_ref[(N_BUCKETS - 1) * H_DIFF + h] * LOG2E
    far_neg = tb_ref[(half - 1) * H_DIFF + h] * LOG2E
    for rb in range(_t5_table_rows(seq) // KEY_BLK):
        d_lo = KEY_BLK * rb - (seq - KEY_BLK)
        if -(d_lo + KEY_BLK - 1) >= MAX_DIST:
            val = jnp.full(blk, far_pos, _F32)
        elif (KEY_BLK - 1) - d_lo <= -MAX_DIST:
            val = jnp.full(blk, far_neg, _F32)
        else:
            d = lax.broadcasted_iota(jnp.int32, blk, 0) + d_lo
            rel = lax.broadcasted_iota(jnp.int32, blk, 1) - d
            n = jnp.abs(rel)
            large = jnp.full_like(n, max_exact)
            for thr in _t5_thresholds():
                large = large + (n >= thr).astype(jnp.int32)
            bucket = jnp.where(rel > 0, half, 0) + jnp.where(n < max_exact, n, large)
            val = jnp.zeros(blk, _F32)
            for j in range(N_BUCKETS):
                val = jnp.where(bucket == j, tb_ref[j * H_DIFF + h] * LOG2E, val)
        o_ref[rb * KEY_BLK:(rb + 1) * KEY_BLK, :] = val.astype(_BF16)


def _t5_table(t5_bias, seq):
    rows = _t5_table_rows(seq)
    return pl.pallas_call(
        lambda tb, o: _t5_table_kernel(tb, o, seq=seq),
        grid=(H_DIFF,),
        in_specs=[pl.BlockSpec(memory_space=pltpu.SMEM)],
        out_specs=pl.BlockSpec((None, rows, KEY_BLK), lambda h: (h, 0, 0)),
        out_shape=jax.ShapeDtypeStruct((H_DIFF, rows, KEY_BLK), _BF16),
        name="t5_table",
    )(t5_bias.reshape(-1))


def _na_first_key_row(r, rows, kr):
    return min(max(r - kr // 2, 0), rows - kr)


def _na_window_start(g, rows, kr):
    return min(max(g * NA_ROWS - kr // 2, 0), rows - NA_KROWS)


def _na_variant_groups(rows):
    return (0, 1, rows // NA_ROWS - 1)


def _na_table_kernel(rpb_ref, o_ref, *, rows, kr):
    h = pl.program_id(0)
    n_dr = 2 * WIN_R - 1
    n_dc = 2 * WIN_C - 1
    pair = (GRID_W, 2 * GRID_W)
    cq = lax.broadcasted_iota(jnp.int32, pair, 0)
    lane = lax.broadcasted_iota(jnp.int32, pair, 1)
    ck = lane & (GRID_W - 1)
    dc = jnp.clip(ck - cq, -(WIN_C - 1), WIN_C - 1) + (WIN_C - 1)
    cs = jnp.clip(cq - WIN_C // 2, 0, GRID_W - WIN_C)
    col_ok = (ck >= cs) & (ck < cs + WIN_C)
    left = lane < GRID_W
    neg = jnp.full(pair, NEG, _F32)
    tiles = {}

    def tile(dr):
        if dr not in tiles:
            acc = jnp.zeros(pair, _F32)
            for j in range(n_dc):
                acc = jnp.where(dc == j, rpb_ref[(h * n_dr + dr) * n_dc + j] * LOG2E, acc)
            tiles[dr] = jnp.where(col_ok, acc, NEG)
        return tiles[dr]

    for v, g in enumerate(_na_variant_groups(rows)):
        ws = _na_window_start(g, rows, kr)
        for rr in range(NA_ROWS):
            r = g * NA_ROWS + rr
            rs = _na_first_key_row(r, rows, kr)
            for kp in range(NA_KROWS // 2):
                parts = []
                for ko in (2 * kp, 2 * kp + 1):
                    in_win = 0 <= ws + ko - rs < kr
                    parts.append(tile(ws + ko - r + WIN_R - 1) if in_win else neg)
                val = parts[0] if parts[0] is parts[1] else jnp.where(left, parts[0], parts[1])
                o_ref[v, rr * GRID_W:(rr + 1) * GRID_W, kp * 2 * GRID_W:(kp + 1) * 2 * GRID_W] = val


def _na_table(rpb, rows, kr):
    n_var = len(_na_variant_groups(rows))
    shape = (n_var, H_NA, NA_ROWS * GRID_W, NA_KROWS * GRID_W)
    return pl.pallas_call(
        lambda r, o: _na_table_kernel(r, o, rows=rows, kr=kr),
        grid=(H_NA,),
        in_specs=[pl.BlockSpec(memory_space=pltpu.SMEM)],
        out_specs=pl.BlockSpec((n_var, None) + shape[2:], lambda h: (0, h, 0, 0)),
        out_shape=jax.ShapeDtypeStruct(shape, _F32),
        name="na_table",
    )(rpb.reshape(-1))


def _cast_weights_once(w_ref, w_scr):
    @pl.when(pl.program_id(0) == 0)
    def _():
        rows = w_ref.shape[0]
        for r0 in range(0, rows, CAST_ROWS):
            w_scr[r0:r0 + CAST_ROWS, :] = w_ref[r0:r0 + CAST_ROWS, :].astype(_BF16)


def _proj_kernel(x_ref, ng_ref, w_ref, seg_ref, gain_ref, o_ref, k12_ref, vt_ref, w_scr):
    _cast_weights_once(w_ref, w_scr)
    x = x_ref[...]
    ms = jnp.mean(x * x, axis=-1, keepdims=True)
    xn = (x * lax.rsqrt(ms + EPS) * ng_ref[...]).astype(_BF16)
    seg = seg_ref[...]
    half_w = GROUP_W // 2
    for j in range(D_IN // GROUP_W):
        sl = slice(j * GROUP_W, (j + 1) * GROUP_W)
        acc = jnp.dot(xn, w_scr[:, sl], preferred_element_type=_F32)
        if j in NORM_GROUPS:
            sq = (acc * acc).astype(_BF16)
            ss = jnp.concatenate(
                [jnp.dot(sq[:, :half_w], seg, preferred_element_type=_F32),
                 jnp.dot(sq[:, half_w:], seg, preferred_element_type=_F32)], axis=-1)
            acc = acc * lax.rsqrt(ss * (1.0 / HEAD_DIM) + EPS) * gain_ref[:, sl]
        elif j in GATE_GROUPS:
            acc = acc * jax.nn.sigmoid(acc)
        o_ref[:, sl] = acc.astype(_BF16)
        if j == KA_GROUP:
            lane = lax.broadcasted_iota(jnp.int32, acc.shape, 1) & (DV_DIFF - 1)
            k12_ref[:, :W_A] = jnp.where(lane < DH_DIFF, acc, 0.0).astype(_BF16)
            k12_ref[:, W_A:] = jnp.where(lane >= DH_DIFF, acc, 0.0).astype(_BF16)
        elif j == VA_GROUP:
            vt_ref[...] = acc.T.astype(_BF16)


def _resident(shape):
    return pl.BlockSpec(shape, lambda i: (0,) * len(shape), pipeline_mode=pl.Buffered(1))


def _proj(x2, norm_g, w, seg, gains, seq):
    n = x2.shape[0]
    per_seq = seq // TM_PROJ
    return pl.pallas_call(
        _proj_kernel,
        grid=(n // TM_PROJ,),
        in_specs=[
            pl.BlockSpec((TM_PROJ, D_MODEL), lambda i: (i, 0)),
            pl.BlockSpec((1, D_MODEL), lambda i: (0, 0)),
            _resident((D_MODEL, D_IN)),
            pl.BlockSpec(seg.shape, lambda i: (0, 0)),
            pl.BlockSpec((1, D_IN), lambda i: (0, 0)),
        ],
        out_specs=[
            pl.BlockSpec((TM_PROJ, D_IN), lambda i: (i, 0)),
            pl.BlockSpec((TM_PROJ, 2 * W_A), lambda i: (i, 0)),
            pl.BlockSpec((None, W_A, TM_PROJ), lambda i: (i // per_seq, 0, i % per_seq)),
        ],
        out_shape=[
            jax.ShapeDtypeStruct((n, D_IN), _BF16),
            jax.ShapeDtypeStruct((n, 2 * W_A), _BF16),
            jax.ShapeDtypeStruct((n // seq, W_A, seq), _BF16),
        ],
        scratch_shapes=[pltpu.VMEM((D_MODEL, D_IN), _BF16)],
        compiler_params=pltpu.CompilerParams(
            dimension_semantics=("arbitrary",), vmem_limit_bytes=VMEM_LIMIT),
        name="proj",
    )(x2, norm_g, w, seg, gains)


def _fold_rows(x, op):
    parts = [x[i * SUBLANES:(i + 1) * SUBLANES] for i in range(x.shape[0] // SUBLANES)]
    while len(parts) > 1:
        parts = [op(parts[i], parts[i + 1]) if i + 1 < len(parts) else parts[i]
                 for i in range(0, len(parts), 2)]
    return parts[0]


def _diff_kernel(lam_ref, q_ref, k1_ref, k2_ref, vt_ref, z_ref, tab_ref, g_ref, o_ref,
                 s_even, s_odd, mx_even, mx_odd, l_even, l_odd,
                 e1_even, e2_even, e1_odd, e2_odd, *, seq, n_steps, lam_init):
    nq = seq // TQ
    nkb = seq // KEY_BLK
    t = pl.program_id(0)
    t_a = jnp.minimum(t, n_steps - 1)
    i_a = t_a % nq
    h_a = (t_a // nq) % H_DIFF
    nt = (((1,), (1,)), ((), ()))

    @pl.when(t == 0)
    def _():
        s_odd[...] = jnp.zeros_like(s_odd)
        mx_odd[...] = jnp.zeros_like(mx_odd)
        e1_even[...] = jnp.zeros_like(e1_even)
        e2_even[...] = jnp.zeros_like(e2_even)
        l_even[...] = jnp.ones_like(l_even)

    def stages(s_w, mx_w, s_r, mx_r, e1_w, e2_w, l_w, e1_r, e2_r, l_r):
        q = q_ref[...]
        row = lax.broadcasted_iota(jnp.int32, (2 * KEY_BLK, KEY_BLK), 0) & (KEY_BLK - 1)
        col = lax.broadcasted_iota(jnp.int32, (2 * KEY_BLK, KEY_BLK), 1)
        onehot = (row == col).astype(_BF16)

        def score_block(j, mx):
            off = pl.multiple_of(i_a * TQ + (seq - KEY_BLK - KEY_BLK * j), KEY_BLK)
            qa = jnp.concatenate([q, tab_ref[h_a, pl.ds(off, TQ), :]], axis=1)
            rows = slice(j * KEY_BLK, (j + 1) * KEY_BLK)
            ka = jnp.concatenate(
                [jnp.concatenate([k1_ref[rows, :], k2_ref[rows, :]], axis=0), onehot], axis=1)
            sj = lax.dot_general(ka, qa, nt, preferred_element_type=_F32)
            s_w[2 * j * KEY_BLK:(2 * j + 2) * KEY_BLK, :] = sj
            p1 = _fold_rows(sj[:KEY_BLK], jnp.maximum)
            p2 = _fold_rows(sj[KEY_BLK:], jnp.maximum)
            return (p1, p2) if mx is None else (jnp.maximum(mx[0], p1), jnp.maximum(mx[1], p2))

        m1 = jnp.max(mx_r[0], axis=0, keepdims=True)
        m2 = jnp.max(mx_r[1], axis=0, keepdims=True)

        def exp_block(j, ls):
            e1 = jnp.exp2(s_r[2 * j * KEY_BLK:(2 * j + 1) * KEY_BLK, :] - m1)
            e2 = jnp.exp2(s_r[(2 * j + 1) * KEY_BLK:(2 * j + 2) * KEY_BLK, :] - m2)
            e1_w[j * KEY_BLK:(j + 1) * KEY_BLK, :] = e1.astype(_BF16)
            e2_w[j * KEY_BLK:(j + 1) * KEY_BLK, :] = e2.astype(_BF16)
            p1 = _fold_rows(e1, jnp.add)
            p2 = _fold_rows(e2, jnp.add)
            return (p1, p2) if ls is None else (ls[0] + p1, ls[1] + p2)

        lv = lam_ref[...]
        lam = (jnp.exp(jnp.sum(lv[0:1] * lv[1:2], axis=-1, keepdims=True))
               - jnp.exp(jnp.sum(lv[2:3] * lv[3:4], axis=-1, keepdims=True)) + lam_init)
        l1 = jnp.sum(l_r[0], axis=0, keepdims=True)
        l2 = jnp.sum(l_r[1], axis=0, keepdims=True)
        r = (lam * l1 / l2).astype(_BF16)

        def pv_chunk(c, acc):
            rows = slice(c * PV_CHUNK, (c + 1) * PV_CHUNK)
            a_t = e1_r[rows, :] - r * e2_r[rows, :]
            part = jnp.dot(vt_ref[:, rows], a_t, preferred_element_type=_F32)
            return part if acc is None else acc + part

        def epilogue(acc):
            o_t = acc * (1.0 / l1)
            ms = jnp.mean(o_t * o_t, axis=0, keepdims=True)
            o = (o_t * lax.rsqrt(ms + EPS)).T
            o = o * g_ref[...] * (1.0 - lam_init)
            o_ref[...] = (o * z_ref[...].astype(_F32)).astype(_BF16)

        n_pv = seq // PV_CHUNK
        mx, ls, acc = None, None, None
        for j in range(nkb):
            ls = exp_block(j, ls)
            if j < n_pv:
                acc = pv_chunk(j, acc)
            if j == n_pv - 1:
                epilogue(acc)
            mx = score_block(j, mx)
        mx_w[0], mx_w[1] = mx
        l_w[0], l_w[1] = ls

    @pl.when(t % 2 == 0)
    def _():
        stages(s_even, mx_even, s_odd, mx_odd, e1_odd, e2_odd, l_odd, e1_even, e2_even, l_even)

    @pl.when(t % 2 == 1)
    def _():
        stages(s_odd, mx_odd, s_even, mx_even, e1_even, e2_even, l_even, e1_odd, e2_odd, l_odd)


def _diff_attention(proj3, k12, vt, lam_vecs, tab, subln_g, lam_init):
    b, seq, _ = proj3.shape
    nq = seq // TQ
    n_steps = b * H_DIFF * nq

    def split(t):
        return t // (H_DIFF * nq), (t // nq) % H_DIFF, t % nq

    def cur(col0, whole):
        def index(t):
            bi, h, i = split(jnp.minimum(t, n_steps - 1))
            return bi, (0 if whole else i), col0 + h
        return index

    def prev(col0, whole):
        def index(t):
            bi, h, i = split(jnp.clip(t - 2, 0, n_steps - 1))
            return bi, (0 if whole else i), col0 + h
        return index

    s_shape = pltpu.VMEM((2 * seq, TQ), _F32)
    e_shape = pltpu.VMEM((seq, TQ), _BF16)
    stat_shape = pltpu.VMEM((2, SUBLANES, TQ), _F32)
    return pl.pallas_call(
        lambda *refs: _diff_kernel(*refs, seq=seq, n_steps=n_steps, lam_init=lam_init),
        grid=(n_steps + 2,),
        in_specs=[
            pl.BlockSpec(lam_vecs.shape, lambda t: (0, 0)),
            pl.BlockSpec((None, TQ, DV_DIFF), cur(0, False)),
            pl.BlockSpec((None, seq, DV_DIFF), cur(0, True)),
            pl.BlockSpec((None, seq, DV_DIFF), cur(H_DIFF, True)),
            pl.BlockSpec((None, DV_DIFF, seq),
                         lambda t: (prev(0, True)(t)[0], prev(0, True)(t)[2], 0)),
            pl.BlockSpec((None, TQ, DV_DIFF), prev(3 * H_DIFF, False)),
            pl.BlockSpec(tab.shape, lambda t: (0, 0, 0)),
            pl.BlockSpec((1, DV_DIFF), lambda t: (0, 0)),
        ],
        out_specs=pl.BlockSpec((None, TQ, DV_DIFF), prev(0, False)),
        out_shape=jax.ShapeDtypeStruct((b, seq, W_A), _BF16),
        scratch_shapes=[
            s_shape, s_shape, stat_shape, stat_shape, stat_shape, stat_shape,
            e_shape, e_shape, e_shape, e_shape,
        ],
        compiler_params=pltpu.CompilerParams(
            dimension_semantics=("arbitrary",), vmem_limit_bytes=VMEM_LIMIT),
        name="diff_attn",
    )(lam_vecs, proj3, k12, k12, vt, proj3, tab, subln_g)


def _nbr_kernel(q_ref, k_ref, v_ref, z_ref, bm_ref, o_ref, ke_scr, ko_scr, *, rows, kr):
    g = pl.program_id(1)
    pair_w = 2 * DH_NA

    @pl.when(g == 0)
    def _():
        k = k_ref[...]
        lane = lax.broadcasted_iota(jnp.int32, k.shape, 1) & (pair_w - 1)
        zero = jnp.zeros_like(k)
        ke_scr[...] = jnp.where(lane < DH_NA, k, zero)
        ko_scr[...] = jnp.where(lane >= DH_NA, k, zero)

    nt = (((1,), (1,)), ((), ()))
    win = NA_KROWS * GRID_W
    tok = NA_ROWS * GRID_W
    ws = jnp.clip(g * NA_ROWS - kr // 2, 0, rows - NA_KROWS)
    start = pl.multiple_of(ws * GRID_W, GRID_W)
    out_lane = lax.broadcasted_iota(jnp.int32, (tok, pair_w), 1)

    def scores(h):
        csl = slice((h // 2) * pair_w, (h // 2 + 1) * pair_w)
        kw = (ke_scr, ko_scr)[h % 2][pl.ds(start, win), csl]
        s = lax.dot_general(q_ref[:, csl], kw, nt, preferred_element_type=_F32)
        s = s + bm_ref[h]
        e = jnp.exp2(s - jnp.max(s, axis=-1, keepdims=True))
        return e.astype(_BF16), 1.0 / jnp.sum(e, axis=-1, keepdims=True)

    def weighted_values(h, prob):
        e, inv = prob
        csl = slice((h // 2) * pair_w, (h // 2 + 1) * pair_w)
        return jnp.dot(e, v_ref[pl.ds(start, win), csl], preferred_element_type=_F32) * inv

    probs, outs = {}, {}
    for step in range(H_NA + NA_LEAD):
        if step < H_NA:
            probs[step] = scores(step)
        h = step - NA_LEAD
        if h >= 0:
            outs[h] = weighted_values(h, probs.pop(h))
            if h % 2 == 1:
                csl = slice((h // 2) * pair_w, (h // 2 + 1) * pair_w)
                o = jnp.where(out_lane < DH_NA, outs.pop(h - 1), outs.pop(h))
                o_ref[:, csl] = (o * z_ref[:, csl].astype(_F32)).astype(_BF16)


def _nbr_attention(proj3, bm_tab, kr):
    b, seq, _ = proj3.shape
    rows = seq // GRID_W
    n_grp = rows // NA_ROWS
    tok = NA_ROWS * GRID_W
    sig = lambda g: tuple(
        (0 <= _na_window_start(g, rows, kr) + ko - _na_first_key_row(g * NA_ROWS + rr, rows, kr) < kr,
         _na_window_start(g, rows, kr) + ko - (g * NA_ROWS + rr))
        for rr in range(NA_ROWS) for ko in range(NA_KROWS))
    variant_of = lambda g: (g > 0) + (g == n_grp - 1)
    reps = _na_variant_groups(rows)
    assert all(sig(g) == sig(reps[variant_of(g)]) for g in range(n_grp))
    grp = lambda gi: (lambda bi, g: (bi, g, gi))
    full = lambda gi: (lambda bi, g: (bi, 0, gi))
    return pl.pallas_call(
        lambda *refs: _nbr_kernel(*refs, rows=rows, kr=kr),
        grid=(b, n_grp),
        in_specs=[
            pl.BlockSpec((None, tok, W_B), grp(4)),
            pl.BlockSpec((None, seq, W_B), full(5)),
            pl.BlockSpec((None, seq, W_B), full(6)),
            pl.BlockSpec((None, tok, W_B), grp(7)),
            pl.BlockSpec((None,) + bm_tab.shape[1:],
                         lambda bi, g: ((g > 0).astype(jnp.int32) + (g == n_grp - 1).astype(jnp.int32),
                                        0, 0, 0)),
        ],
        out_specs=pl.BlockSpec((None, tok, W_B), lambda bi, g: (bi, g, 0)),
        out_shape=jax.ShapeDtypeStruct((b, seq, W_B), _BF16),
        scratch_shapes=[pltpu.VMEM((seq, W_B), _BF16), pltpu.VMEM((seq, W_B), _BF16)],
        compiler_params=pltpu.CompilerParams(
            dimension_semantics=("arbitrary", "arbitrary"), vmem_limit_bytes=VMEM_LIMIT),
        name="nbr_attn",
    )(proj3, proj3, proj3, proj3, bm_tab)


def _out_kernel(x_ref, ya_ref, yb_ref, p_ref, wo_ref, wg_ref, wp_ref, o_ref,
                wo_scr, wg_scr, wp_scr):
    _cast_weights_once(wo_ref, wo_scr)
    _cast_weights_once(wg_ref, wg_scr)
    _cast_weights_once(wp_ref, wp_scr)
    y = (jnp.dot(ya_ref[...], wo_scr[:W_A, :], preferred_element_type=_F32)
         + jnp.dot(yb_ref[...], wo_scr[W_A:, :], preferred_element_type=_F32))
    x1 = x_ref[...] + y
    gate = jax.nn.sigmoid(jnp.dot(x1.astype(_BF16), wg_scr[...], preferred_element_type=_F32))
    ple = jnp.dot(p_ref[...].astype(_BF16), wp_scr[...], preferred_element_type=_F32)
    o_ref[...] = x1 + gate * ple


def _out(x2, ya2, yb2, p2, wo, wg, wp):
    n = x2.shape[0]
    tile = lambda w: pl.BlockSpec((TM_OUT, w), lambda i: (i, 0))
    return pl.pallas_call(
        _out_kernel,
        grid=(n // TM_OUT,),
        in_specs=[tile(D_MODEL), tile(W_A), tile(W_B), tile(PLE_DIM),
                  _resident(wo.shape), _resident(wg.shape), _resident(wp.shape)],
        out_specs=tile(D_MODEL),
        out_shape=jax.ShapeDtypeStruct((n, D_MODEL), _F32),
        scratch_shapes=[pltpu.VMEM(w.shape, _BF16) for w in (wo, wg, wp)],
        compiler_params=pltpu.CompilerParams(
            dimension_semantics=("arbitrary",), vmem_limit_bytes=VMEM_LIMIT),
        name="out_proj",
    )(x2, ya2, yb2, p2, wo, wg, wp)


def _segment_ones():
    idx = np.arange(GROUP_W // 2) // HEAD_DIM
    return jnp.asarray((idx[:, None] == idx[None, :]).astype(np.float32), dtype=_BF16)


def kernel(x, p, norm_g, w_in, w_out, q_norm_a, k_norm_a, lam_q1, lam_k1, lam_q2, lam_k2,
           subln_g, t5_bias, q_norm_b, k_norm_b, na_rpb, w_ple_gate, w_ple_proj):
    b, seq, _ = x.shape
    depth = w_in.shape[0]
    n = b * seq
    rows = seq // GRID_W
    kr = min(WIN_R, rows)
    assert kr == WIN_R and rows % NA_ROWS == 0 and seq % TQ == 0 and n % TM_PROJ == 0

    seg = _segment_ones()
    bias_tab = _t5_table(t5_bias.astype(_F32), seq)
    ones_w = jnp.ones((1, GROUP_W), _F32)
    for i in range(depth):
        lam_init = 0.8 - 0.6 * math.exp(-0.3 * i)
        tile_a = lambda g, s: jnp.tile(g.astype(_F32), GROUP_W // DH_DIFF)[None, :] * s
        gains = jnp.concatenate([
            tile_a(q_norm_a[i], DH_DIFF ** -0.5 * LOG2E), tile_a(k_norm_a[i], 1.0), ones_w, ones_w,
            tile_a(q_norm_b[i], DH_NA ** -0.5 * LOG2E), tile_a(k_norm_b[i], 1.0), ones_w, ones_w],
            axis=-1)
        proj, k12, vt = _proj(x.reshape(n, D_MODEL), norm_g[i][None, :].astype(_F32),
                              w_in[i].astype(_F32), seg, gains, seq)
        proj3 = proj.reshape(b, seq, D_IN)
        lam_vecs = jnp.stack([lam_q1[i], lam_k1[i], lam_q2[i], lam_k2[i]]).astype(_F32)
        ya = _diff_attention(proj3, k12.reshape(b, seq, 2 * W_A), vt, lam_vecs, bias_tab,
                             subln_g[i][None, :].astype(_F32), lam_init)
        bm_tab = _na_table(na_rpb[i].astype(_F32), rows, kr)
        yb = _nbr_attention(proj3, bm_tab, kr)
        x = _out(x.reshape(n, D_MODEL), ya.reshape(n, W_A), yb.reshape(n, W_B),
                 p[i].reshape(n, PLE_DIM), w_out[i].astype(_F32),
                 w_ple_gate[i].astype(_F32), w_ple_proj[i].astype(_F32)).reshape(b, seq, D_MODEL)
    return x
```

```python
import math

import jax
import jax.numpy as jnp
import numpy as np
from jax import lax
from jax.experimental import pallas as pl
from jax.experimental.pallas import tpu as pltpu

D_MODEL = 1024
PLE_DIM = 256
GRID_W = 64
H_DIFF = 4
DH_DIFF = 64
DV_DIFF = 2 * DH_DIFF
H_NA = 8
DH_NA = 64
WIN_R = 8
WIN_C = 16
N_BUCKETS = 32
MAX_DIST = 128
W_A = H_DIFF * DV_DIFF
W_B = H_NA * DH_NA
D_IN = 4 * W_A + 4 * W_B
EPS = 1e-6
NEG = -1e30

GROUP_W = 512
NORM_GROUPS = (0, 1, 4, 5)
GATE_GROUPS = (3, 7)
QA_GROUP, KA_GROUP, VA_GROUP = 0, 1, 2
HEAD_DIM = 64

TM_PROJ = 512
TM_OUT = 512
CAST_ROWS = 128
TQ = 512
KEY_BLK = 128
PV_CHUNK = 256
SUBLANES = 8
NA_ROWS = 4
NA_KROWS = NA_ROWS + WIN_R
NA_LEAD = 2
LOG2E = math.log2(math.e)
VMEM_LIMIT = 56 * 1024 * 1024

_F32 = jnp.float32
_BF16 = jnp.bfloat16


def _t5_thresholds():
    half = N_BUCKETS // 2
    max_exact = half // 2
    steps = half - max_exact
    ratio = MAX_DIST // max_exact
    thr = []
    for j in range(1, steps):
        n = max_exact
        while n ** steps < (max_exact ** steps) * (ratio ** j):
            n += 1
        thr.append(n)
    return tuple(thr)


def _t5_table_rows(seq):
    return 2 * seq - KEY_BLK


def _t5_table_kernel(tb_ref, o_ref, *, seq):
    h = pl.program_id(0)
    half = N_BUCKETS // 2
    max_exact = half // 2
    blk = (KEY_BLK, KEY_BLK)
    far_pos = tb_ref[(N_BUCKETS - 1) * H_DIFF + h] * LOG2E
    far_neg = tb_ref[(half - 1) * H_DIFF + h] * LOG2E
    for rb in range(_t5_table_rows(seq) // KEY_BLK):
        d_lo = KEY_BLK * rb - (seq - KEY_BLK)
        if -(d_lo + KEY_BLK - 1) >= MAX_DIST:
            val = jnp.full(blk, far_pos, _F32)
        elif (KEY_BLK - 1) - d_lo <= -MAX_DIST:
            val = jnp.full(blk, far_neg, _F32)
        else:
            d = lax.broadcasted_iota(jnp.int32, blk, 1) + d_lo
            rel = lax.broadcasted_iota(jnp.int32, blk, 0) - d
            n = jnp.abs(rel)
            large = jnp.full_like(n, max_exact)
            for thr in _t5_thresholds():
                large = large + (n >= thr).astype(jnp.int32)
            bucket = jnp.where(rel > 0, half, 0) + jnp.where(n < max_exact, n, large)
            val = jnp.zeros(blk, _F32)
            for j in range(N_BUCKETS):
                val = jnp.where(bucket == j, tb_ref[j * H_DIFF + h] * LOG2E, val)
        o_ref[:, rb * KEY_BLK:(rb + 1) * KEY_BLK] = val.astype(_BF16)


def _t5_table(t5_bias, seq):
    rows = _t5_table_rows(seq)
    return pl.pallas_call(
        lambda tb, o: _t5_table_kernel(tb, o, seq=seq),
        grid=(H_DIFF,),
        in_specs=[pl.BlockSpec(memory_space=pltpu.SMEM)],
        out_specs=pl.BlockSpec((None, KEY_BLK, rows), lambda h: (h, 0, 0)),
        out_shape=jax.ShapeDtypeStruct((H_DIFF, KEY_BLK, rows), _BF16),
        name="t5_table",
    )(t5_bias.reshape(-1))


def _na_first_key_row(r, rows, kr):
    return min(max(r - kr // 2, 0), rows - kr)


def _na_window_start(g, rows, kr):
    return min(max(g * NA_ROWS - kr // 2, 0), rows - NA_KROWS)


def _na_variant_groups(rows):
    return (0, 1, rows // NA_ROWS - 1)


def _na_table_kernel(rpb_ref, o_ref, *, rows, kr):
    h = pl.program_id(0)
    n_dr = 2 * WIN_R - 1
    n_dc = 2 * WIN_C - 1
    pair = (GRID_W, 2 * GRID_W)
    cq = lax.broadcasted_iota(jnp.int32, pair, 0)
    lane = lax.broadcasted_iota(jnp.int32, pair, 1)
    ck = lane & (GRID_W - 1)
    dc = jnp.clip(ck - cq, -(WIN_C - 1), WIN_C - 1) + (WIN_C - 1)
    cs = jnp.clip(cq - WIN_C // 2, 0, GRID_W - WIN_C)
    col_ok = (ck >= cs) & (ck < cs + WIN_C)
    left = lane < GRID_W
    neg = jnp.full(pair, NEG, _F32)
    tiles = {}

    def tile(dr):
        if dr not in tiles:
            acc = jnp.zeros(pair, _F32)
            for j in range(n_dc):
                acc = jnp.where(dc == j, rpb_ref[(h * n_dr + dr) * n_dc + j] * LOG2E, acc)
            tiles[dr] = jnp.where(col_ok, acc, NEG)
        return tiles[dr]

    for v, g in enumerate(_na_variant_groups(rows)):
        ws = _na_window_start(g, rows, kr)
        for rr in range(NA_ROWS):
            r = g * NA_ROWS + rr
            rs = _na_first_key_row(r, rows, kr)
            for kp in range(NA_KROWS // 2):
                parts = []
                for ko in (2 * kp, 2 * kp + 1):
                    in_win = 0 <= ws + ko - rs < kr
                    parts.append(tile(ws + ko - r + WIN_R - 1) if in_win else neg)
                val = parts[0] if parts[0] is parts[1] else jnp.where(left, parts[0], parts[1])
                o_ref[v, rr * GRID_W:(rr + 1) * GRID_W, kp * 2 * GRID_W:(kp + 1) * 2 * GRID_W] = val


def _na_table(rpb, rows, kr):
    n_var = len(_na_variant_groups(rows))
    shape = (n_var, H_NA, NA_ROWS * GRID_W, NA_KROWS * GRID_W)
    return pl.pallas_call(
        lambda r, o: _na_table_kernel(r, o, rows=rows, kr=kr),
        grid=(H_NA,),
        in_specs=[pl.BlockSpec(memory_space=pltpu.SMEM)],
        out_specs=pl.BlockSpec((n_var, None) + shape[2:], lambda h: (0, h, 0, 0)),
        out_shape=jax.ShapeDtypeStruct(shape, _F32),
        name="na_table",
    )(rpb.reshape(-1))


def _cast_weights_once(w_ref, w_scr):
    @pl.when(pl.program_id(0) == 0)
    def _():
        rows = w_ref.shape[0]
        for r0 in range(0, rows, CAST_ROWS):
            w_scr[r0:r0 + CAST_ROWS, :] = w_ref[r0:r0 + CAST_ROWS, :].astype(_BF16)


def _proj_kernel(x_ref, ng_ref, w_ref, seg_ref, gain_ref, o_ref, k12_ref, qt_ref, vt_ref, w_scr):
    _cast_weights_once(w_ref, w_scr)
    x = x_ref[...]
    ms = jnp.mean(x * x, axis=-1, keepdims=True)
    xn = (x * lax.rsqrt(ms + EPS) * ng_ref[...]).astype(_BF16)
    seg = seg_ref[...]
    half_w = GROUP_W // 2
    for j in range(D_IN // GROUP_W):
        sl = slice(j * GROUP_W, (j + 1) * GROUP_W)
        acc = jnp.dot(xn, w_scr[:, sl], preferred_element_type=_F32)
        if j in NORM_GROUPS:
            sq = (acc * acc).astype(_BF16)
            ss = jnp.concatenate(
                [jnp.dot(sq[:, :half_w], seg, preferred_element_type=_F32),
                 jnp.dot(sq[:, half_w:], seg, preferred_element_type=_F32)], axis=-1)
            acc = acc * lax.rsqrt(ss * (1.0 / HEAD_DIM) + EPS) * gain_ref[:, sl]
        elif j in GATE_GROUPS:
            acc = acc * jax.nn.sigmoid(acc)
        o_ref[:, sl] = acc.astype(_BF16)
        if j == KA_GROUP:
            lane = lax.broadcasted_iota(jnp.int32, acc.shape, 1) & (DV_DIFF - 1)
            k12_ref[:, :W_A] = jnp.where(lane < DH_DIFF, acc, 0.0).astype(_BF16)
            k12_ref[:, W_A:] = jnp.where(lane >= DH_DIFF, acc, 0.0).astype(_BF16)
        elif j == QA_GROUP:
            qt_ref[...] = acc.T.astype(_BF16)
        elif j == VA_GROUP:
            vt_ref[...] = acc.T.astype(_BF16)


def _resident(shape):
    return pl.BlockSpec(shape, lambda i: (0,) * len(shape), pipeline_mode=pl.Buffered(1))


def _proj(x2, norm_g, w, seg, gains, seq):
    n = x2.shape[0]
    per_seq = seq // TM_PROJ
    return pl.pallas_call(
        _proj_kernel,
        grid=(n // TM_PROJ,),
        in_specs=[
            pl.BlockSpec((TM_PROJ, D_MODEL), lambda i: (i, 0)),
            pl.BlockSpec((1, D_MODEL), lambda i: (0, 0)),
            _resident((D_MODEL, D_IN)),
            pl.BlockSpec(seg.shape, lambda i: (0, 0)),
            pl.BlockSpec((1, D_IN), lambda i: (0, 0)),
        ],
        out_specs=[
            pl.BlockSpec((TM_PROJ, D_IN), lambda i: (i, 0)),
            pl.BlockSpec((TM_PROJ, 2 * W_A), lambda i: (i, 0)),
            pl.BlockSpec((None, W_A, TM_PROJ), lambda i: (i // per_seq, 0, i % per_seq)),
            pl.BlockSpec((None, W_A, TM_PROJ), lambda i: (i // per_seq, 0, i % per_seq)),
        ],
        out_shape=[
            jax.ShapeDtypeStruct((n, D_IN), _BF16),
            jax.ShapeDtypeStruct((n, 2 * W_A), _BF16),
            jax.ShapeDtypeStruct((n // seq, W_A, seq), _BF16),
            jax.ShapeDtypeStruct((n // seq, W_A, seq), _BF16),
        ],
        scratch_shapes=[pltpu.VMEM((D_MODEL, D_IN), _BF16)],
        compiler_params=pltpu.CompilerParams(
            dimension_semantics=("arbitrary",), vmem_limit_bytes=VMEM_LIMIT),
        name="proj",
    )(x2, norm_g, w, seg, gains)


def _fold_rows(x, op):
    parts = [x[i * SUBLANES:(i + 1) * SUBLANES] for i in range(x.shape[0] // SUBLANES)]
    while len(parts) > 1:
        parts = [op(parts[i], parts[i + 1]) if i + 1 < len(parts) else parts[i]
                 for i in range(0, len(parts), 2)]
    return parts[0]


def _diff_kernel(lam_ref, qt_ref, k1_ref, k2_ref, vt_ref, z_ref, tab_ref, g_ref, o_ref,
                 s_even, s_odd, mx_even, mx_odd, l_even, l_odd,
                 e1_even, e2_even, e1_odd, e2_odd, *, seq, n_steps, lam_init):
    nq = seq // TQ
    nkb = seq // KEY_BLK
    t = pl.program_id(0)
    t_a = jnp.minimum(t, n_steps - 1)
    i_a = t_a % nq
    h_a = (t_a // nq) % H_DIFF
    nt = (((1,), (1,)), ((), ()))

    @pl.when(t == 0)
    def _():
        s_odd[...] = jnp.zeros_like(s_odd)
        mx_odd[...] = jnp.zeros_like(mx_odd)
        e1_even[...] = jnp.zeros_like(e1_even)
        e2_even[...] = jnp.zeros_like(e2_even)
        l_even[...] = jnp.ones_like(l_even)

    def stages(s_w, mx_w, s_r, mx_r, e1_w, e2_w, l_w, e1_r, e2_r, l_r):
        qt = qt_ref[...]
        row = lax.broadcasted_iota(jnp.int32, (2 * KEY_BLK, KEY_BLK), 0) & (KEY_BLK - 1)
        col = lax.broadcasted_iota(jnp.int32, (2 * KEY_BLK, KEY_BLK), 1)
        onehot = (row == col).astype(_BF16)

        def score_block(j, mx):
            off = pl.multiple_of(i_a * TQ + (seq - KEY_BLK - KEY_BLK * j), KEY_BLK)
            qa = jnp.concatenate([qt, tab_ref[h_a, :, pl.ds(off, TQ)]], axis=0)
            rows = slice(j * KEY_BLK, (j + 1) * KEY_BLK)
            ka = jnp.concatenate(
                [jnp.concatenate([k1_ref[rows, :], k2_ref[rows, :]], axis=0), onehot], axis=1)
            sj = jnp.dot(ka, qa, preferred_element_type=_F32)
            s_w[2 * j * KEY_BLK:(2 * j + 2) * KEY_BLK, :] = sj
            p1 = _fold_rows(sj[:KEY_BLK], jnp.maximum)
            p2 = _fold_rows(sj[KEY_BLK:], jnp.maximum)
            return (p1, p2) if mx is None else (jnp.maximum(mx[0], p1), jnp.maximum(mx[1], p2))

        m1 = jnp.max(mx_r[0], axis=0, keepdims=True)
        m2 = jnp.max(mx_r[1], axis=0, keepdims=True)

        def exp_block(j, ls):
            e1 = jnp.exp2(s_r[2 * j * KEY_BLK:(2 * j + 1) * KEY_BLK, :] - m1)
            e2 = jnp.exp2(s_r[(2 * j + 1) * KEY_BLK:(2 * j + 2) * KEY_BLK, :] - m2)
            e1_w[j * KEY_BLK:(j + 1) * KEY_BLK, :] = e1.astype(_BF16)
            e2_w[j * KEY_BLK:(j + 1) * KEY_BLK, :] = e2.astype(_BF16)
            p1 = _fold_rows(e1, jnp.add)
            p2 = _fold_rows(e2, jnp.add)
            return (p1, p2) if ls is None else (ls[0] + p1, ls[1] + p2)

        lv = lam_ref[...]
        lam = (jnp.exp(jnp.sum(lv[0:1] * lv[1:2], axis=-1, keepdims=True))
               - jnp.exp(jnp.sum(lv[2:3] * lv[3:4], axis=-1, keepdims=True)) + lam_init)
        l1 = jnp.sum(l_r[0], axis=0, keepdims=True)
        l2 = jnp.sum(l_r[1], axis=0, keepdims=True)
        r = (lam * l1 / l2).astype(_BF16)

        def pv_chunk(c, acc):
            rows = slice(c * PV_CHUNK, (c + 1) * PV_CHUNK)
            a_t = e1_r[rows, :] - r * e2_r[rows, :]
            part = jnp.dot(vt_ref[:, rows], a_t, preferred_element_type=_F32)
            return part if acc is None else acc + part

        def epilogue(acc):
            o_t = acc * (1.0 / l1)
            ms = jnp.mean(o_t * o_t, axis=0, keepdims=True)
            o = (o_t * lax.rsqrt(ms + EPS)).T
            o = o * g_ref[...] * (1.0 - lam_init)
            o_ref[...] = (o * z_ref[...].astype(_F32)).astype(_BF16)

        n_pv = seq // PV_CHUNK
        mx, ls, acc = None, None, None
        for j in range(nkb):
            ls = exp_block(j, ls)
            if j < n_pv:
                acc = pv_chunk(j, acc)
            if j == n_pv - 1:
                epilogue(acc)
            mx = score_block(j, mx)
        mx_w[0], mx_w[1] = mx
        l_w[0], l_w[1] = ls

    @pl.when(t % 2 == 0)
    def _():
        stages(s_even, mx_even, s_odd, mx_odd, e1_odd, e2_odd, l_odd, e1_even, e2_even, l_even)

    @pl.when(t % 2 == 1)
    def _():
        stages(s_odd, mx_odd, s_even, mx_even, e1_even, e2_even, l_even, e1_odd, e2_odd, l_odd)


def _diff_attention(proj3, k12, qt, vt, lam_vecs, tab, subln_g, lam_init):
    b, seq, _ = proj3.shape
    nq = seq // TQ
    n_steps = b * H_DIFF * nq

    def split(t):
        return t // (H_DIFF * nq), (t // nq) % H_DIFF, t % nq

    def cur(col0, whole):
        def index(t):
            bi, h, i = split(jnp.minimum(t, n_steps - 1))
            return bi, (0 if whole else i), col0 + h
        return index

    def prev(col0, whole):
        def index(t):
            bi, h, i = split(jnp.clip(t - 2, 0, n_steps - 1))
            return bi, (0 if whole else i), col0 + h
        return index

    s_shape = pltpu.VMEM((2 * seq, TQ), _F32)
    e_shape = pltpu.VMEM((seq, TQ), _BF16)
    stat_shape = pltpu.VMEM((2, SUBLANES, TQ), _F32)
    return pl.pallas_call(
        lambda *refs: _diff_kernel(*refs, seq=seq, n_steps=n_steps, lam_init=lam_init),
        grid=(n_steps + 2,),
        in_specs=[
            pl.BlockSpec(lam_vecs.shape, lambda t: (0, 0)),
            pl.BlockSpec((None, DV_DIFF, TQ),
                         lambda t: (cur(0, False)(t)[0], cur(0, False)(t)[2], cur(0, False)(t)[1])),
            pl.BlockSpec((None, seq, DV_DIFF), cur(0, True)),
            pl.BlockSpec((None, seq, DV_DIFF), cur(H_DIFF, True)),
            pl.BlockSpec((None, DV_DIFF, seq),
                         lambda t: (prev(0, True)(t)[0], prev(0, True)(t)[2], 0)),
            pl.BlockSpec((None, TQ, DV_DIFF), prev(3 * H_DIFF, False)),
            pl.BlockSpec(tab.shape, lambda t: (0, 0, 0)),
            pl.BlockSpec((1, DV_DIFF), lambda t: (0, 0)),
        ],
        out_specs=pl.BlockSpec((None, TQ, DV_DIFF), prev(0, False)),
        out_shape=jax.ShapeDtypeStruct((b, seq, W_A), _BF16),
        scratch_shapes=[
            s_shape, s_shape, stat_shape, stat_shape, stat_shape, stat_shape,
            e_shape, e_shape, e_shape, e_shape,
        ],
        compiler_params=pltpu.CompilerParams(
            dimension_semantics=("arbitrary",), vmem_limit_bytes=VMEM_LIMIT),
        name="diff_attn",
    )(lam_vecs, qt, k12, k12, vt, proj3, tab, subln_g)


def _nbr_kernel(q_ref, k_ref, v_ref, z_ref, bm_ref, o_ref, ke_scr, ko_scr, *, rows, kr):
    g = pl.program_id(1)
    pair_w = 2 * DH_NA

    @pl.when(g == 0)
    def _():
        k = k_ref[...]
        lane = lax.broadcasted_iota(jnp.int32, k.shape, 1) & (pair_w - 1)
        zero = jnp.zeros_like(k)
        ke_scr[...] = jnp.where(lane < DH_NA, k, zero)
        ko_scr[...] = jnp.where(lane >= DH_NA, k, zero)

    nt = (((1,), (1,)), ((), ()))
    win = NA_KROWS * GRID_W
    tok = NA_ROWS * GRID_W
    ws = jnp.clip(g * NA_ROWS - kr // 2, 0, rows - NA_KROWS)
    start = pl.multiple_of(ws * GRID_W, GRID_W)
    out_lane = lax.broadcasted_iota(jnp.int32, (tok, pair_w), 1)

    def scores(h):
        csl = slice((h // 2) * pair_w, (h // 2 + 1) * pair_w)
        kw = (ke_scr, ko_scr)[h % 2][pl.ds(start, win), csl]
        s = lax.dot_general(q_ref[:, csl], kw, nt, preferred_element_type=_F32)
        s = s + bm_ref[h]
        e = jnp.exp2(s - jnp.max(s, axis=-1, keepdims=True))
        return e.astype(_BF16), 1.0 / jnp.sum(e, axis=-1, keepdims=True)

    def weighted_values(h, prob):
        e, inv = prob
        csl = slice((h // 2) * pair_w, (h // 2 + 1) * pair_w)
        return jnp.dot(e, v_ref[pl.ds(start, win), csl], preferred_element_type=_F32) * inv

    probs, outs = {}, {}
    for step in range(H_NA + NA_LEAD):
        if step < H_NA:
            probs[step] = scores(step)
        h = step - NA_LEAD
        if h >= 0:
            outs[h] = weighted_values(h, probs.pop(h))
            if h % 2 == 1:
                csl = slice((h // 2) * pair_w, (h // 2 + 1) * pair_w)
                o = jnp.where(out_lane < DH_NA, outs.pop(h - 1), outs.pop(h))
                o_ref[:, csl] = (o * z_ref[:, csl].astype(_F32)).astype(_BF16)


def _nbr_attention(proj3, bm_tab, kr):
    b, seq, _ = proj3.shape
    rows = seq // GRID_W
    n_grp = rows // NA_ROWS
    tok = NA_ROWS * GRID_W
    sig = lambda g: tuple(
        (0 <= _na_window_start(g, rows, kr) + ko - _na_first_key_row(g * NA_ROWS + rr, rows, kr) < kr,
         _na_window_start(g, rows, kr) + ko - (g * NA_ROWS + rr))
        for rr in range(NA_ROWS) for ko in range(NA_KROWS))
    variant_of = lambda g: (g > 0) + (g == n_grp - 1)
    reps = _na_variant_groups(rows)
    assert all(sig(g) == sig(reps[variant_of(g)]) for g in range(n_grp))
    grp = lambda gi: (lambda bi, g: (bi, g, gi))
    full = lambda gi: (lambda bi, g: (bi, 0, gi))
    return pl.pallas_call(
        lambda *refs: _nbr_kernel(*refs, rows=rows, kr=kr),
        grid=(b, n_grp),
        in_specs=[
            pl.BlockSpec((None, tok, W_B), grp(4)),
            pl.BlockSpec((None, seq, W_B), full(5)),
            pl.BlockSpec((None, seq, W_B), full(6)),
            pl.BlockSpec((None, tok, W_B), grp(7)),
            pl.BlockSpec((None,) + bm_tab.shape[1:],
                         lambda bi, g: ((g > 0).astype(jnp.int32) + (g == n_grp - 1).astype(jnp.int32),
                                        0, 0, 0)),
        ],
        out_specs=pl.BlockSpec((None, tok, W_B), lambda bi, g: (bi, g, 0)),
        out_shape=jax.ShapeDtypeStruct((b, seq, W_B), _BF16),
        scratch_shapes=[pltpu.VMEM((seq, W_B), _BF16), pltpu.VMEM((seq, W_B), _BF16)],
        compiler_params=pltpu.CompilerParams(
            dimension_semantics=("arbitrary", "arbitrary"), vmem_limit_bytes=VMEM_LIMIT),
        name="nbr_attn",
    )(proj3, proj3, proj3, proj3, bm_tab)


def _out_kernel(x_ref, ya_ref, yb_ref, p_ref, wo_ref, wg_ref, wp_ref, o_ref,
                wo_scr, wg_scr, wp_scr):
    _cast_weights_once(wo_ref, wo_scr)
    _cast_weights_once(wg_ref, wg_scr)
    _cast_weights_once(wp_ref, wp_scr)
    y = (jnp.dot(ya_ref[...], wo_scr[:W_A, :], preferred_element_type=_F32)
         + jnp.dot(yb_ref[...], wo_scr[W_A:, :], preferred_element_type=_F32))
    x1 = x_ref[...] + y
    gate = jax.nn.sigmoid(jnp.dot(x1.astype(_BF16), wg_scr[...], preferred_element_type=_F32))
    ple = jnp.dot(p_ref[...].astype(_BF16), wp_scr[...], preferred_element_type=_F32)
    o_ref[...] = x1 + gate * ple


def _out(x2, ya2, yb2, p2, wo, wg, wp):
    n = x2.shape[0]
    tile = lambda w: pl.BlockSpec((TM_OUT, w), lambda i: (i, 0))
    return pl.pallas_call(
        _out_kernel,
        grid=(n // TM_OUT,),
        in_specs=[tile(D_MODEL), tile(W_A), tile(W_B), tile(PLE_DIM),
                  _resident(wo.shape), _resident(wg.shape), _resident(wp.shape)],
        out_specs=tile(D_MODEL),
        out_shape=jax.ShapeDtypeStruct((n, D_MODEL), _F32),
        scratch_shapes=[pltpu.VMEM(w.shape, _BF16) for w in (wo, wg, wp)],
        compiler_params=pltpu.CompilerParams(
            dimension_semantics=("arbitrary",), vmem_limit_bytes=VMEM_LIMIT),
        name="out_proj",
    )(x2, ya2, yb2, p2, wo, wg, wp)


def _segment_ones():
    idx = np.arange(GROUP_W // 2) // HEAD_DIM
    return jnp.asarray((idx[:, None] == idx[None, :]).astype(np.float32), dtype=_BF16)


def kernel(x, p, norm_g, w_in, w_out, q_norm_a, k_norm_a, lam_q1, lam_k1, lam_q2, lam_k2,
           subln_g, t5_bias, q_norm_b, k_norm_b, na_rpb, w_ple_gate, w_ple_proj):
    b, seq, _ = x.shape
    depth = w_in.shape[0]
    n = b * seq
    rows = seq // GRID_W
    kr = min(WIN_R, rows)
    assert kr == WIN_R and rows % NA_ROWS == 0 and seq % TQ == 0 and n % TM_PROJ == 0

    seg = _segment_ones()
    bias_tab = _t5_table(t5_bias.astype(_F32), seq)
    ones_w = jnp.ones((1, GROUP_W), _F32)
    for i in range(depth):
        lam_init = 0.8 - 0.6 * math.exp(-0.3 * i)
        tile_a = lambda g, s: jnp.tile(g.astype(_F32), GROUP_W // DH_DIFF)[None, :] * s
        gains = jnp.concatenate([
            tile_a(q_norm_a[i], DH_DIFF ** -0.5 * LOG2E), tile_a(k_norm_a[i], 1.0), ones_w, ones_w,
            tile_a(q_norm_b[i], DH_NA ** -0.5 * LOG2E), tile_a(k_norm_b[i], 1.0), ones_w, ones_w],
            axis=-1)
        proj, k12, qt, vt = _proj(x.reshape(n, D_MODEL), norm_g[i][None, :].astype(_F32),
                                  w_in[i].astype(_F32), seg, gains, seq)
        proj3 = proj.reshape(b, seq, D_IN)
        lam_vecs = jnp.stack([lam_q1[i], lam_k1[i], lam_q2[i], lam_k2[i]]).astype(_F32)
        ya = _diff_attention(proj3, k12.reshape(b, seq, 2 * W_A), qt, vt, lam_vecs, bias_tab,
                             subln_g[i][None, :].astype(_F32), lam_init)
        bm_tab = _na_table(na_rpb[i].astype(_F32), rows, kr)
        yb = _nbr_attention(proj3, bm_tab, kr)
        x = _out(x.reshape(n, D_MODEL), ya.reshape(n, W_A), yb.reshape(n, W_B),
                 p[i].reshape(n, PLE_DIM), w_out[i].astype(_F32),
                 w_ple_gate[i].astype(_F32), w_ple_proj[i].astype(_F32)).reshape(b, seq, D_MODEL)
    return x
```
